```python
import math
import jax
import jax.numpy as jnp
from jax import lax
import numpy as np

D_MODEL = 1024
BATCH = 4
SEQ = 8192
DEPTH = 4

N_EVEN = (DEPTH + 1) // 2
N_ODD = DEPTH // 2
RMS_EPS = 1e-6

GDN_HEADS = 4
GDN_DK = 128
GDN_DV = 128
GDN_QK_W = GDN_HEADS * GDN_DK
GDN_V_W = GDN_HEADS * GDN_DV
GDN_CONV = 4
GDN_CHUNK = 64

POOL_WINDOWS = (2, 4, 8, 16)
POOL_GROUPS = len(POOL_WINDOWS)
POOL_GROUP_W = 128
POOL_W = POOL_GROUPS * POOL_GROUP_W

EVEN_QKV = 2 * GDN_QK_W + GDN_V_W
EVEN_IN = EVEN_QKV + GDN_V_W + 2 * GDN_HEADS + POOL_W
EVEN_OUT = GDN_V_W + POOL_W

DIL_PATTERNS = ((128, 1), (512, 4), (2048, 16))
ATT_GROUPS = len(DIL_PATTERNS)
ATT_HEADS = 8
ATT_DH = 128
ATT_W = ATT_HEADS * ATT_DH
ODD_IN = ATT_GROUPS * 3 * ATT_W
ATT_BLOCK = 128

D_FF = 2816
FFN_CONV = 3

kernel_name = "hybrid_gdn_pool_dilated_alibi_convffn"


def rms_norm(x, gain):
    xf = x.astype(jnp.float32)
    y = xf * lax.rsqrt(jnp.mean(xf * xf, axis=-1, keepdims=True) + RMS_EPS)
    return (y * gain.astype(jnp.float32)).astype(x.dtype)


def l2_norm(x):
    xf = x.astype(jnp.float32)
    return xf * lax.rsqrt(jnp.sum(xf * xf, axis=-1, keepdims=True) + RMS_EPS)


def causal_dwconv(x, w):
    K, C = w.shape
    return lax.conv_general_dilated(
        x, w[:, None, :].astype(x.dtype), window_strides=(1,), padding=[(K - 1, 0)],
        dimension_numbers=("NWC", "WIO", "NWC"), feature_group_count=C)


def gated_delta_rule(q, k, v, g, beta):
    B, T, H, dk = q.shape
    dv = v.shape[-1]
    C = GDN_CHUNK
    N = T // C
    f32 = jnp.float32

    def chunks(t):
        return t.astype(f32).reshape(B, N, C, H, -1).transpose(0, 3, 1, 2, 4)

    q = chunks(q) * (dk ** -0.5)
    k = chunks(k)
    v = chunks(v)
    g = g.astype(f32).reshape(B, N, C, H).transpose(0, 3, 1, 2)
    beta = beta.astype(f32).reshape(B, N, C, H).transpose(0, 3, 1, 2)
    gc = jnp.cumsum(g, axis=-1)

    idx = jnp.arange(C)
    causal = idx[:, None] >= idx[None, :]
    strict = idx[:, None] > idx[None, :]
    diff = gc[..., :, None] - gc[..., None, :]
    decay = jnp.where(causal, jnp.exp(jnp.where(causal, diff, 0.0)), 0.0)

    kb = k * beta[..., None]
    L = jnp.where(strict, jnp.einsum("bhncd,bhnsd->bhncs", kb, k) * decay, 0.0)
    rhs = jnp.concatenate([v * beta[..., None], kb * jnp.exp(gc)[..., None]], axis=-1)
    uw = lax.linalg.triangular_solve(L, rhs, left_side=True, lower=True,
                                     unit_diagonal=True)
    u, w = uw[..., :dv], uw[..., dv:]

    intra = jnp.where(causal, jnp.einsum("bhncd,bhnsd->bhncs", q, k) * decay, 0.0)
    qg = q * jnp.exp(gc)[..., None]
    kdec = k * jnp.exp(gc[..., -1:] - gc)[..., None]
    glast = jnp.exp(gc[..., -1])

    def step(S, xs):
        qg_i, kdec_i, u_i, w_i, intra_i, gl_i = xs
        v_new = u_i - jnp.einsum("bhck,bhkv->bhcv", w_i, S)
        o_i = (jnp.einsum("bhck,bhkv->bhcv", qg_i, S)
               + jnp.einsum("bhcs,bhsv->bhcv", intra_i, v_new))
        S = S * gl_i[..., None, None] + jnp.einsum("bhck,bhcv->bhkv", kdec_i, v_new)
        return S, o_i

    xs = tuple(jnp.moveaxis(t, 2, 0) for t in (qg, kdec, u, w, intra, glast))
    S0 = jnp.zeros((B, H, dk, dv), f32)
    _, o = lax.scan(step, S0, xs)
    return o.transpose(1, 0, 3, 2, 4).reshape(B, T, H, dv)


def multiscale_pool(p, pool_w, pool_scale):
    B, T, _ = p.shape
    pf = p.astype(jnp.float32).reshape(B, T, POOL_GROUPS, POOL_GROUP_W)
    csum = jnp.cumsum(pf, axis=1)
    t1 = jnp.arange(1, T + 1, dtype=jnp.float32)
    pooled = []
    for gi, win in enumerate(POOL_WINDOWS):
        cg = csum[:, :, gi]
        lag = jnp.pad(cg, ((0, 0), (win, 0), (0, 0)))[:, :T]
        cnt = jnp.minimum(t1, float(win))[None, :, None]
        pooled.append((cg - lag) / cnt)
    pooled = jnp.stack(pooled, axis=2) - pf
    y = jnp.einsum("btgc,gcd->btgd", pooled, pool_w.astype(jnp.float32))
    return (y.reshape(B, T, POOL_W) * pool_scale.astype(jnp.float32)).astype(p.dtype)


def even_mixer(h, w_in, w_out, conv_w, a_log, dt_bias, gdn_norm, pool_w, pool_scale):
    B, T, _ = h.shape
    proj = h @ w_in
    i1 = EVEN_QKV
    i2 = i1 + GDN_V_W
    i3 = i2 + GDN_HEADS
    i4 = i3 + GDN_HEADS
    qkv, z, b_raw, a_raw, pool_in = jnp.split(proj, [i1, i2, i3, i4], axis=-1)

    qkv = jax.nn.silu(causal_dwconv(qkv, conv_w))
    q, k, v = jnp.split(qkv, [GDN_QK_W, 2 * GDN_QK_W], axis=-1)
    q = l2_norm(q.reshape(B, T, GDN_HEADS, GDN_DK))
    k = l2_norm(k.reshape(B, T, GDN_HEADS, GDN_DK))
    v = v.reshape(B, T, GDN_HEADS, GDN_DV)
    beta = jax.nn.sigmoid(b_raw.astype(jnp.float32))
    g = -jnp.exp(a_log.astype(jnp.float32)) * jax.nn.softplus(
        a_raw.astype(jnp.float32) + dt_bias.astype(jnp.float32))
    o = gated_delta_rule(q, k, v, g, beta)
    o = rms_norm(o, gdn_norm) * jax.nn.silu(
        z.astype(jnp.float32).reshape(B, T, GDN_HEADS, GDN_DV))
    o_a = o.reshape(B, T, GDN_V_W).astype(h.dtype)

    o_b = multiscale_pool(pool_in, pool_w, pool_scale)
    return jnp.concatenate([o_a, o_b], axis=-1) @ w_out


def alibi_slopes(n_heads):
    return jnp.exp2(-8.0 * jnp.arange(1, n_heads + 1, dtype=jnp.float32) / n_heads)


def dilated_band_attention(q, k, v, dil, n_back, slopes):
    B, T, H, dh = q.shape
    L = T // dil
    nb = -(-L // ATT_BLOCK)
    Lp = nb * ATT_BLOCK

    def to_sub(t):
        t = t.reshape(B, L, dil, H, dh).transpose(0, 2, 3, 1, 4)
        t = jnp.pad(t, ((0, 0), (0, 0), (0, 0), (0, Lp - L), (0, 0)))
        return t.reshape(B, dil, H, nb, ATT_BLOCK, dh)

    def band(t):
        prev = jnp.pad(t, ((0, 0), (0, 0), (0, 0), (1, 0), (0, 0), (0, 0)))[:, :, :, :nb]
        return jnp.concatenate([prev, t], axis=4)

    qb = to_sub(q)
    kb = band(to_sub(k))
    vb = band(to_sub(v))

    a = jnp.arange(ATT_BLOCK)[:, None]
    j = jnp.arange(2 * ATT_BLOCK)[None, :]
    rel = ATT_BLOCK + a - j
    blk = jnp.arange(nb)[:, None, None]
    mask = (rel >= 0) & (rel <= n_back) & ((j >= ATT_BLOCK) | (blk > 0))
    bias = -(slopes * dil)[:, None, None, None] * rel.astype(jnp.float32)

    s = jnp.einsum("brhnqd,brhnkd->brhnqk", qb, kb) + bias
    s = jnp.where(mask, s, -jnp.inf)
    m = jnp.max(s, axis=-1, keepdims=True)
    p = jnp.exp(s - m)
    l = jnp.sum(p, axis=-1, keepdims=True)
    o = jnp.einsum("brhnqk,brhnkd->brhnqd", p, vb) / l
    lse = m + jnp.log(l)

    def from_sub(t):
        t = t.reshape(B, dil, H, Lp, -1)[:, :, :, :L]
        return t.transpose(0, 3, 1, 2, 4).reshape(B, T, H, -1)

    return from_sub(o), from_sub(lse)[..., 0]


def odd_mixer(h, w_in, w_out, q_norm, k_norm):
    B, T, _ = h.shape
    slopes = alibi_slopes(ATT_HEADS)
    outs, lses = [], []
    for gi, (window, dil) in enumerate(DIL_PATTERNS):
        cols = w_in[:, gi * 3 * ATT_W:(gi + 1) * 3 * ATT_W]
        proj = (h @ cols).astype(jnp.float32).reshape(B, T, 3, ATT_HEADS, ATT_DH)
        q = rms_norm(proj[:, :, 0], q_norm) * (ATT_DH ** -0.5)
        k = rms_norm(proj[:, :, 1], k_norm)
        v = proj[:, :, 2]
        o, lse = dilated_band_attention(q, k, v, dil, window // dil, slopes)
        outs.append(o)
        lses.append(lse)
    wts = jax.nn.softmax(jnp.stack(lses), axis=0)
    o = jnp.sum(wts[..., None] * jnp.stack(outs), axis=0)
    return o.reshape(B, T, ATT_W).astype(h.dtype) @ w_out


def conv_ffn(h, w_up, conv_w, conv_b, w_down):
    up = h @ w_up
    gate, val = jnp.split(up, 2, axis=-1)
    gate = causal_dwconv(gate, conv_w) + conv_b
    return (jax.nn.silu(gate) * val) @ w_down


def setup_inputs(seed: int = 0) -> dict:
    key = jax.random.key(seed)
    ks = jax.random.split(key, 24)
    f32 = jnp.float32
    D = D_MODEL

    def nrm(k, shape, s):
        return jax.random.normal(k, shape, f32) * s

    dt = jnp.exp(jax.random.uniform(ks[10], (N_EVEN, GDN_HEADS), f32,
                                    math.log(1e-3), math.log(1e-1)))
    return {
        "x": nrm(ks[0], (BATCH, SEQ, D), 1.0),
        "c": nrm(ks[1], (BATCH, D), 1.0),
        "ada_w": nrm(ks[2], (DEPTH, D, 6 * D), 0.5 * D ** -0.5),
        "ada_b": nrm(ks[3], (DEPTH, 6 * D), 0.01),
        "norm_mix": 1.0 + nrm(ks[4], (DEPTH, D), 0.02),
        "norm_ffn": 1.0 + nrm(ks[5], (DEPTH, D), 0.02),
        "ev_w_in": nrm(ks[6], (N_EVEN, D, EVEN_IN), D ** -0.5),
        "ev_w_out": nrm(ks[7], (N_EVEN, EVEN_OUT, D), EVEN_OUT ** -0.5),
        "gdn_conv_w": nrm(ks[8], (N_EVEN, GDN_CONV, EVEN_QKV), GDN_CONV ** -0.5),
        "gdn_a_log": jnp.log(jax.random.uniform(ks[9], (N_EVEN, GDN_HEADS), f32, 1.0, 16.0)),
        "gdn_dt_bias": dt + jnp.log(-jnp.expm1(-dt)),
        "gdn_norm": 1.0 + nrm(ks[11], (N_EVEN, GDN_DV), 0.02),
        "pool_w": nrm(ks[12], (N_EVEN, POOL_GROUPS, POOL_GROUP_W, POOL_GROUP_W), POOL_GROUP_W ** -0.5),
        "pool_scale": 1.0 + nrm(ks[13], (N_EVEN, POOL_W), 0.1),
        "od_w_in": nrm(ks[14], (N_ODD, D, ODD_IN), D ** -0.5),
        "od_w_out": nrm(ks[15], (N_ODD, ATT_W, D), ATT_W ** -0.5),
        "att_q_norm": 1.0 + nrm(ks[16], (N_ODD, ATT_DH), 0.02),
        "att_k_norm": 1.0 + nrm(ks[17], (N_ODD, ATT_DH), 0.02),
        "ffn_w_up": nrm(ks[18], (DEPTH, D, 2 * D_FF), D ** -0.5),
        "ffn_conv_w": nrm(ks[19], (DEPTH, FFN_CONV, D_FF), FFN_CONV ** -0.5),
        "ffn_conv_b": nrm(ks[20], (DEPTH, D_FF), 0.01),
        "ffn_w_down": nrm(ks[21], (DEPTH, D_FF, D), D_FF ** -0.5),
    }


def reference(x, c, ada_w, ada_b, norm_mix, norm_ffn, ev_w_in, ev_w_out, gdn_conv_w,
              gdn_a_log, gdn_dt_bias, gdn_norm, pool_w, pool_scale, od_w_in, od_w_out,
              att_q_norm, att_k_norm, ffn_w_up, ffn_conv_w, ffn_conv_b, ffn_w_down):
    cs = jax.nn.silu(c)
    for i in range(DEPTH):
        mod = (cs @ ada_w[i] + ada_b[i])[:, None, :]
        sh_m, sc_m, g_m, sh_f, sc_f, g_f = jnp.split(mod, 6, axis=-1)

        hm = rms_norm(x, norm_mix[i]) * (1.0 + sc_m) + sh_m
        if i % 2 == 0:
            e = i // 2
            y = even_mixer(hm, ev_w_in[e], ev_w_out[e], gdn_conv_w[e], gdn_a_log[e],
                           gdn_dt_bias[e], gdn_norm[e], pool_w[e], pool_scale[e])
        else:
            o = i // 2
            y = odd_mixer(hm, od_w_in[o], od_w_out[o], att_q_norm[o], att_k_norm[o])
        x = x + g_m * y

        hf = rms_norm(x, norm_ffn[i]) * (1.0 + sc_f) + sh_f
        x = x + g_f * conv_ffn(hf, ffn_w_up[i], ffn_conv_w[i], ffn_conv_b[i], ffn_w_down[i])
    return x
```

```python
import functools

import jax
import jax.numpy as jnp
from jax import lax
from jax.experimental import pallas as pl
from jax.experimental.pallas import tpu as pltpu

F32 = jnp.float32
BF16 = jnp.bfloat16

RMS_EPS = 1e-6

GDN_HEADS = 4
GDN_DK = 128
GDN_DV = 128
GDN_QK_W = GDN_HEADS * GDN_DK
GDN_V_W = GDN_HEADS * GDN_DV
GDN_QKV_W = 2 * GDN_QK_W + GDN_V_W
GDN_CONV = 4
GDN_CHUNK = 64
GDN_TILE = 2 * GDN_CHUNK
GDN_INV_BLOCK = 16

POOL_WINDOWS = (2, 4, 8, 16)
POOL_GROUP_W = 128
POOL_W = len(POOL_WINDOWS) * POOL_GROUP_W
POOL_HALO = 16

DIL_PATTERNS = ((128, 1), (512, 4), (2048, 16))
ATT_HEADS = 8
ATT_DH = 128
ATT_W = ATT_HEADS * ATT_DH
ATT_BLOCK = 128
ATT_SLOPES = tuple(2.0 ** (-8.0 * (h + 1) / ATT_HEADS) for h in range(ATT_HEADS))
MASKED_SCORE = -1e30

FFN_CONV = 3
FFN_HALO = 16
LANES = 128
ROW_TILE = 512
VMEM_LIMIT = 56 * 1024 * 1024


def _sigmoid(x):
    return 1.0 / (1.0 + jnp.exp(-x))


def _silu(x):
    return x * _sigmoid(x)


def _softplus(x):
    return jnp.maximum(x, 0.0) + jnp.log(1.0 + jnp.exp(-jnp.abs(x)))


def _mm(a, b):
    return jnp.dot(a, b, preferred_element_type=F32)


def _mm_nt(a, b):
    return lax.dot_general(a, b, (((1,), (1,)), ((), ())), preferred_element_type=F32)


def _mod_norm(x, gain, shift, scale):
    ms = jnp.mean(x * x, axis=-1, keepdims=True)
    return x * lax.rsqrt(ms + RMS_EPS) * (gain * (1.0 + scale)) + shift


def _params(semantics):
    return pltpu.CompilerParams(dimension_semantics=semantics, vmem_limit_bytes=VMEM_LIMIT)


def _resident(shape):
    nd = len(shape)
    return pl.BlockSpec(shape, lambda *_: (0,) * nd, pipeline_mode=pl.Buffered(1))


def _ada_kernel(c_ref, w_ref, b_ref, o_ref):
    cs = _silu(c_ref[...])
    o_ref[...] = _mm(cs.astype(BF16), w_ref[...].astype(BF16)) + b_ref[...]


def _ada(c, ada_w, ada_b):
    depth, d, n = ada_w.shape
    b = c.shape[0]
    rows = 8
    cp = jnp.pad(c, ((0, rows - b), (0, 0)))
    tn = 1536
    out = pl.pallas_call(
        _ada_kernel,
        grid=(depth, n // tn),
        in_specs=[
            pl.BlockSpec((rows, d), lambda l, j: (0, 0)),
            pl.BlockSpec((None, d, tn), lambda l, j: (l, 0, j)),
            pl.BlockSpec((None, 1, tn), lambda l, j: (l, 0, j)),
        ],
        out_specs=pl.BlockSpec((None, rows, tn), lambda l, j: (l, 0, j)),
        out_shape=jax.ShapeDtypeStruct((depth, rows, n), F32),
        compiler_params=_params(("parallel", "parallel")),
        name="ada_mod",
    )(cp, ada_w, ada_b.reshape(depth, 1, n))
    return out[:, :b].reshape(depth, b, 6, d)


def _even_in_kernel(x_ref, mod_ref, gain_ref, w_ref, qkv_ref, z_ref, p_ref, ba_ref, h_scr):
    m = mod_ref[...]
    h_scr[...] = _mod_norm(x_ref[...], gain_ref[...], m[0:1], m[1:2]).astype(BF16)
    col = 0
    for ref in (qkv_ref, z_ref, p_ref, ba_ref):
        width = ref.shape[-1]
        for lo in range(0, width, 512):
            hi = min(lo + 512, width)
            ref[:, lo:hi] = _mm(h_scr[...], w_ref[:, col + lo:col + hi]).astype(ref.dtype)
        col += width


def _even_in(x, mod, gain, w):
    b, t, d = x.shape
    tm = ROW_TILE
    row = lambda width: pl.BlockSpec((None, tm, width), lambda bi, i: (bi, i, 0))
    return pl.pallas_call(
        _even_in_kernel,
        grid=(b, t // tm),
        in_specs=[
            row(d),
            pl.BlockSpec((None, 6, d), lambda bi, i: (bi, 0, 0)),
            pl.BlockSpec((1, d), lambda bi, i: (0, 0)),
            _resident(w.shape),
        ],
        out_specs=[row(GDN_QKV_W), row(GDN_V_W), row(POOL_W), row(LANES)],
        out_shape=[
            jax.ShapeDtypeStruct((b, t, GDN_QKV_W), BF16),
            jax.ShapeDtypeStruct((b, t, GDN_V_W), BF16),
            jax.ShapeDtypeStruct((b, t, POOL_W), BF16),
            jax.ShapeDtypeStruct((b, t, LANES), F32),
        ],
        scratch_shapes=[pltpu.VMEM((tm, d), BF16)],
        compiler_params=_params(("parallel", "parallel")),
        name="even_in_proj",
    )(x, mod, gain, w)


def _lane_bcast(x, lane):
    return jnp.broadcast_to(x[:, lane:lane + 1], (x.shape[0], LANES))


def _row_bcast(x, row, rows):
    return jnp.broadcast_to(x[row:row + 1, :], (rows, x.shape[1]))


def _l2norm(x):
    return x * lax.rsqrt(jnp.sum(x * x, axis=-1, keepdims=True) + RMS_EPS)


def _unit_lower_inverse(low, blk, eye):
    d = jnp.where(blk, low, 0.0)
    n = low - d
    db = d.astype(BF16)
    d2 = _mm(db, db)
    d2b = d2.astype(BF16)
    d4 = _mm(d2b, d2b)
    d4b = d4.astype(BF16)
    d8 = _mm(d4b, d4b)
    acc = _mm((eye - d).astype(BF16), (eye + d2).astype(BF16))
    acc = _mm(acc.astype(BF16), (eye + d4).astype(BF16))
    dinv = _mm(acc.astype(BF16), (eye + d8).astype(BF16))
    dinvb = dinv.astype(BF16)
    m = _mm(dinvb, n.astype(BF16))
    mb = m.astype(BF16)
    m2 = _mm(mb, mb)
    y = _mm((eye - m).astype(BF16), (eye + m2).astype(BF16))
    return _mm(y.astype(BF16), dinvb)


def _gdn_pool_kernel(qkv_ref, z_ref, p_ref, ba_ref, convw_ref, alog_ref, dtb_ref, gnorm_ref,
                     poolw_ref, pscale_ref, o_ref, qkv_ext, p_ext, s_scr):
    tt = GDN_TILE
    ch = GDN_CHUNK
    t = pl.program_id(1)

    @pl.when(t == 0)
    def _():
        qkv_ext[0:8, :] = jnp.zeros((8, GDN_QKV_W), F32)
        p_ext[0:POOL_HALO, :] = jnp.zeros((POOL_HALO, POOL_W), F32)
        s_scr[...] = jnp.zeros(s_scr.shape, F32)

    qkv_ext[8:8 + tt, :] = qkv_ref[...].astype(F32)
    p_ext[POOL_HALO:POOL_HALO + tt, :] = p_ref[...].astype(F32)

    cw = convw_ref[...]
    acc = cw[GDN_CONV - 1:GDN_CONV] * qkv_ext[8:8 + tt, :]
    for j in range(GDN_CONV - 1):
        acc = acc + cw[j:j + 1] * qkv_ext[pl.ds(8 - (GDN_CONV - 1) + j, tt), :]
    qkv = _silu(acc)

    row = lax.broadcasted_iota(jnp.int32, (tt, tt), 0)
    col = lax.broadcasted_iota(jnp.int32, (tt, tt), 1)
    chunk_of = lambda idx: jnp.right_shift(idx, ch.bit_length() - 1)
    block_of = lambda idx: jnp.right_shift(idx, GDN_INV_BLOCK.bit_length() - 1)
    same = chunk_of(row) == chunk_of(col)
    causal = jnp.logical_and(same, row >= col)
    strict = jnp.logical_and(same, row > col)
    blk = block_of(row) == block_of(col)
    eye = jnp.where(row == col, 1.0, 0.0).astype(F32)
    first = row < ch

    ba = ba_ref[...]
    beta_all = _sigmoid(ba)
    g_all = -jnp.exp(alog_ref[...]) * _softplus(ba + dtb_ref[...])
    tri = jnp.where(causal, 1.0, 0.0).astype(F32)
    gc_all = jnp.dot(tri, g_all, preferred_element_type=F32, precision=lax.Precision.HIGHEST)
    gc_all_t = gc_all.T

    z = z_ref[...].astype(F32)
    gnorm = gnorm_ref[...]
    for h in range(GDN_HEADS):
        hs = slice(h * GDN_DK, (h + 1) * GDN_DK)
        q = _l2norm(qkv[:, hs]) * (GDN_DK ** -0.5)
        k = _l2norm(qkv[:, GDN_QK_W + h * GDN_DK:GDN_QK_W + (h + 1) * GDN_DK])
        v = qkv[:, 2 * GDN_QK_W + h * GDN_DV:2 * GDN_QK_W + (h + 1) * GDN_DV]
        beta = _lane_bcast(beta_all, h)
        gcol = _lane_bcast(gc_all, GDN_HEADS + h)
        grow = _row_bcast(gc_all_t, GDN_HEADS + h, tt)
        decay = jnp.where(causal, jnp.exp(jnp.where(causal, gcol - grow, 0.0)), 0.0)
        egc = jnp.exp(gcol)

        kb = k * beta
        k16 = k.astype(BF16)
        low = jnp.where(strict, _mm_nt(kb.astype(BF16), k16) * decay, 0.0)
        intra = (_mm_nt(q.astype(BF16), k16) * decay).astype(BF16)
        ainv = _unit_lower_inverse(low, blk, eye)
        rhs = jnp.concatenate([v * beta, kb * egc], axis=1).astype(BF16)
        uw = _mm(ainv.astype(BF16), rhs)
        u = uw[:, :GDN_DV]
        w = uw[:, GDN_DV:]
        qg = q * egc
        glast = jnp.where(first, _row_bcast(gcol, ch - 1, tt), _row_bcast(gcol, tt - 1, tt))
        kdec = k * jnp.exp(glast - gcol)

        s = s_scr[h]
        outs = []
        for c in range(tt // ch):
            rs = slice(c * ch, (c + 1) * ch)
            lhs = jnp.concatenate([w[rs], qg[rs]], axis=0).astype(BF16)
            proj = _mm(lhs, s.astype(BF16))
            vnew = u[rs] - proj[:ch]
            zeros = jnp.zeros_like(vnew)
            vfull = jnp.concatenate([vnew, zeros] if c == 0 else [zeros, vnew], axis=0)
            outs.append(proj[ch:] + _mm(intra[rs], vfull.astype(BF16)))
            gl = _row_bcast(egc, (c + 1) * ch - 1, GDN_DK)
            s = gl * s + _mm(kdec[rs].T.astype(BF16), vnew.astype(BF16))
        s_scr[h] = s
        o = jnp.concatenate(outs, axis=0)
        o = o * lax.rsqrt(jnp.mean(o * o, axis=-1, keepdims=True) + RMS_EPS) * gnorm
        o_ref[:, hs] = (o * _silu(z[:, hs])).astype(o_ref.dtype)

    tok = (t * tt + 1 + lax.broadcasted_iota(jnp.int32, (tt, POOL_GROUP_W), 0)).astype(F32)
    pscale = pscale_ref[...]
    for gi, win in enumerate(POOL_WINDOWS):
        cs = slice(gi * POOL_GROUP_W, (gi + 1) * POOL_GROUP_W)
        cur = p_ext[POOL_HALO:POOL_HALO + tt, cs]
        tot = cur
        for back in range(1, win):
            tot = tot + p_ext[pl.ds(POOL_HALO - back, tt), cs]
        pooled = tot / jnp.minimum(tok, float(win)) - cur
        y = _mm(pooled.astype(BF16), poolw_ref[gi])
        o_ref[:, GDN_V_W + gi * POOL_GROUP_W:GDN_V_W + (gi + 1) * POOL_GROUP_W] = (
            y * pscale[:, cs]).astype(o_ref.dtype)

    qkv_ext[0:8, :] = qkv_ext[tt:tt + 8, :]
    p_ext[0:POOL_HALO, :] = p_ext[tt:tt + POOL_HALO, :]


def _gdn_pool(qkv, z, p, ba, conv_w, alog_row, dtb_row, gnorm, pool_w, pool_scale):
    b, t, _ = qkv.shape
    tt = GDN_TILE
    row = lambda width: pl.BlockSpec((None, tt, width), lambda bi, i: (bi, i, 0))
    full = lambda a: pl.BlockSpec(a.shape, lambda bi, i: (0,) * a.ndim)
    small = (conv_w, alog_row, dtb_row, gnorm, pool_w, pool_scale)
    return pl.pallas_call(
        _gdn_pool_kernel,
        grid=(b, t // tt),
        in_specs=[row(GDN_QKV_W), row(GDN_V_W), row(POOL_W), row(LANES)] + [full(a) for a in small],
        out_specs=row(GDN_V_W + POOL_W),
        out_shape=jax.ShapeDtypeStruct((b, t, GDN_V_W + POOL_W), BF16),
        scratch_shapes=[
            pltpu.VMEM((tt + 8, GDN_QKV_W), F32),
            pltpu.VMEM((tt + POOL_HALO, POOL_W), F32),
            pltpu.VMEM((GDN_HEADS, GDN_DK, GDN_DV), F32),
        ],
        compiler_params=_params(("parallel", "arbitrary")),
        name="gdn_pool",
    )(qkv, z, p, ba, *small)


def _out_res_kernel(a_ref, w_ref, x_ref, mod_ref, o_ref, *, gate_row):
    gate = mod_ref[...][gate_row:gate_row + 1]
    o_ref[...] = x_ref[...] + gate * _mm(a_ref[...], w_ref[...])


def _out_res(a, w, x, mod, gate_row):
    b, t, d = x.shape
    k = a.shape[-1]
    tm = ROW_TILE
    return pl.pallas_call(
        functools.partial(_out_res_kernel, gate_row=gate_row),
        grid=(b, t // tm),
        in_specs=[
            pl.BlockSpec((None, tm, k), lambda bi, i: (bi, i, 0)),
            _resident(w.shape),
            pl.BlockSpec((None, tm, d), lambda bi, i: (bi, i, 0)),
            pl.BlockSpec((None, 6, d), lambda bi, i: (bi, 0, 0)),
        ],
        out_specs=pl.BlockSpec((None, tm, d), lambda bi, i: (bi, i, 0)),
        out_shape=jax.ShapeDtypeStruct((b, t, d), F32),
        compiler_params=_params(("parallel", "parallel")),
        name="out_proj_residual",
    )(a, w, x, mod)


def _odd_in_kernel(x_ref, mod_ref, gain_ref, w_ref, qn_ref, kn_ref, o_ref, h_scr):
    m = mod_ref[...]
    h_scr[...] = _mod_norm(x_ref[...], gain_ref[...], m[0:1], m[1:2]).astype(BF16)
    qn = qn_ref[...] * (ATT_DH ** -0.5)
    kn = kn_ref[...]
    half = ATT_W // 2
    for blk in range(o_ref.shape[-1] // half):
        lo = blk * half
        y = _mm(h_scr[...], w_ref[:, lo:lo + half])
        part = (lo // ATT_W) % 3
        if part == 2:
            o_ref[:, lo:lo + half] = y.astype(o_ref.dtype)
            continue
        gain = qn if part == 0 else kn
        for h in range(half // ATT_DH):
            yh = y[:, h * ATT_DH:(h + 1) * ATT_DH]
            ms = jnp.mean(yh * yh, axis=-1, keepdims=True)
            o_ref[:, lo + h * ATT_DH:lo + (h + 1) * ATT_DH] = (
                yh * lax.rsqrt(ms + RMS_EPS) * gain).astype(o_ref.dtype)


def _odd_in(x, mod, gain, w, q_norm, k_norm):
    b, t, d = x.shape
    n = w.shape[-1]
    tm = ROW_TILE
    return pl.pallas_call(
        _odd_in_kernel,
        grid=(b, t // tm),
        in_specs=[
            pl.BlockSpec((None, tm, d), lambda bi, i: (bi, i, 0)),
            pl.BlockSpec((None, 6, d), lambda bi, i: (bi, 0, 0)),
            pl.BlockSpec((1, d), lambda bi, i: (0, 0)),
            _resident(w.shape),
            pl.BlockSpec((1, ATT_DH), lambda bi, i: (0, 0)),
            pl.BlockSpec((1, ATT_DH), lambda bi, i: (0, 0)),
        ],
        out_specs=pl.BlockSpec((None, tm, n), lambda bi, i: (bi, i, 0)),
        out_shape=jax.ShapeDtypeStruct((b, t, n), BF16),
        scratch_shapes=[pltpu.VMEM((tm, d), BF16)],
        compiler_params=_params(("parallel", "parallel")),
        name="odd_in_proj",
    )(x, mod, gain, w, q_norm, k_norm)


def _attn_kernel(q_ref, kp_ref, kc_ref, vp_ref, vc_ref, o_ref, lse_ref, *, dil, n_back):
    blk = ATT_BLOCK
    n = pl.program_id(2)
    a = lax.broadcasted_iota(jnp.int32, (blk, blk), 0)
    j = lax.broadcasted_iota(jnp.int32, (blk, blk), 1)
    rel_c = a - j
    rel_p = rel_c + blk
    ok_c = jnp.logical_and(rel_c >= 0, rel_c <= n_back)
    back_p = jnp.where(n > 0, n_back, -1)
    ok_p = jnp.logical_and(rel_p >= 0, rel_p <= back_p)
    relf_c = rel_c.astype(F32)
    relf_p = rel_p.astype(F32)
    lane = lax.broadcasted_iota(jnp.int32, (blk, LANES), 1)
    lse_all = jnp.zeros((blk, LANES), F32)
    for h in range(ATT_HEADS):
        hs = slice(h * ATT_DH, (h + 1) * ATT_DH)
        slope = ATT_SLOPES[h] * dil
        q = q_ref[:, hs]
        s_c = jnp.where(ok_c, _mm_nt(q, kc_ref[:, hs]) - slope * relf_c, MASKED_SCORE)
        s_p = jnp.where(ok_p, _mm_nt(q, kp_ref[:, hs]) - slope * relf_p, MASKED_SCORE)
        m = jnp.maximum(jnp.max(s_c, axis=-1, keepdims=True), jnp.max(s_p, axis=-1, keepdims=True))
        p_c = jnp.exp(s_c - m)
        p_p = jnp.exp(s_p - m)
        l = jnp.sum(p_c, axis=-1, keepdims=True) + jnp.sum(p_p, axis=-1, keepdims=True)
        o = _mm(p_c.astype(BF16), vc_ref[:, hs]) + _mm(p_p.astype(BF16), vp_ref[:, hs])
        o_ref[:, hs] = (o * (1.0 / l)).astype(o_ref.dtype)
        lse_all = jnp.where(lane == h, m + jnp.log(l), lse_all)
    lse_ref[...] = lse_all


def _attention_group(qkv, group, window, dil):
    b, t, width = qkv.shape
    length = t // dil
    nb = length // ATT_BLOCK
    per_tok = width // ATT_W
    view = qkv.reshape(b, length, dil * width)
    base = 3 * group

    def spec(part, prev):
        def index(bi, r, n):
            nn = jnp.maximum(n - 1, 0) if prev else n
            return (bi, nn, r * per_tok + base + part)
        return pl.BlockSpec((None, ATT_BLOCK, ATT_W), index)

    o, lse = pl.pallas_call(
        functools.partial(_attn_kernel, dil=dil, n_back=window // dil),
        grid=(b, dil, nb),
        in_specs=[spec(0, False), spec(1, True), spec(1, False), spec(2, True), spec(2, False)],
        out_specs=[
            pl.BlockSpec((None, ATT_BLOCK, ATT_W), lambda bi, r, n: (bi, n, r)),
            pl.BlockSpec((None, ATT_BLOCK, LANES), lambda bi, r, n: (bi, n, r)),
        ],
        out_shape=[
            jax.ShapeDtypeStruct((b, length, dil * ATT_W), BF16),
            jax.ShapeDtypeStruct((b, length, dil * LANES), F32),
        ],
        compiler_params=_params(("parallel", "parallel", "arbitrary")),
        name=f"dilated_attention_d{dil}",
    )(view, view, view, view, view)
    return o.reshape(b, t, ATT_W), lse.reshape(b, t, LANES)


def _merge_out_kernel(o0_ref, o1_ref, o2_ref, l0_ref, l1_ref, l2_ref, w_ref, x_ref, mod_ref,
                      out_ref, a_scr):
    l0, l1, l2 = l0_ref[...], l1_ref[...], l2_ref[...]
    mx = jnp.maximum(jnp.maximum(l0, l1), l2)
    e0, e1, e2 = jnp.exp(l0 - mx), jnp.exp(l1 - mx), jnp.exp(l2 - mx)
    inv = 1.0 / (e0 + e1 + e2)
    w0, w1, w2 = e0 * inv, e1 * inv, e2 * inv
    for h in range(ATT_HEADS):
        hs = slice(h * ATT_DH, (h + 1) * ATT_DH)
        merged = (_lane_bcast(w0, h) * o0_ref[:, hs].astype(F32)
                  + _lane_bcast(w1, h) * o1_ref[:, hs].astype(F32)
                  + _lane_bcast(w2, h) * o2_ref[:, hs].astype(F32))
        a_scr[:, hs] = merged.astype(BF16)
    gate = mod_ref[...][2:3]
    out_ref[...] = x_ref[...] + gate * _mm(a_scr[...], w_ref[...])


def _merge_out(outs, lses, w, x, mod):
    b, t, d = x.shape
    tm = ROW_TILE
    row = lambda width: pl.BlockSpec((None, tm, width), lambda bi, i: (bi, i, 0))
    return pl.pallas_call(
        _merge_out_kernel,
        grid=(b, t // tm),
        in_specs=[row(ATT_W)] * 3 + [row(LANES)] * 3 + [
            _resident(w.shape), row(d), pl.BlockSpec((None, 6, d), lambda bi, i: (bi, 0, 0))],
        out_specs=row(d),
        out_shape=jax.ShapeDtypeStruct((b, t, d), F32),
        scratch_shapes=[pltpu.VMEM((tm, ATT_W), BF16)],
        compiler_params=_params(("parallel", "parallel")),
        name="attn_merge_out_proj",
    )(*outs, *lses, w, x, mod)


def _ffn_up_kernel(x_ref, xh_ref, mod_ref, gain_ref, wg_ref, wv_ref, cw_ref, cb_ref, o_ref,
                   h_scr, g_scr, *, chunk):
    tm = x_ref.shape[0]
    halo = FFN_HALO
    m = mod_ref[...]
    gain = gain_ref[...]
    shift, scale = m[3:4], m[4:5]
    h_scr[halo:, :] = _mod_norm(x_ref[...], gain, shift, scale).astype(BF16)
    live = jnp.where(pl.program_id(1) > 0, 1.0, 0.0)
    h_scr[0:halo, :] = (_mod_norm(xh_ref[...], gain, shift, scale) * live).astype(BF16)
    cw = cw_ref[...]
    cb = cb_ref[...]
    for lo in range(0, o_ref.shape[-1], chunk):
        cs = slice(lo, lo + chunk)
        g_scr[...] = _mm(h_scr[...], wg_ref[:, cs])
        val = _mm(h_scr[halo:, :], wv_ref[:, cs])
        gate = cb[:, cs] + cw[FFN_CONV - 1:FFN_CONV, cs] * g_scr[halo:, :]
        for j in range(FFN_CONV - 1):
            gate = gate + cw[j:j + 1, cs] * g_scr[pl.ds(halo - (FFN_CONV - 1) + j, tm), :]
        o_ref[:, cs] = (_silu(gate) * val).astype(o_ref.dtype)


def _ffn_up(x, mod, gain, wg, wv, conv_w, conv_b):
    b, t, d = x.shape
    f = wg.shape[-1]
    tm = ROW_TILE
    chunk = 256
    per = tm // FFN_HALO
    return pl.pallas_call(
        functools.partial(_ffn_up_kernel, chunk=chunk),
        grid=(b, t // tm),
        in_specs=[
            pl.BlockSpec((None, tm, d), lambda bi, i: (bi, i, 0)),
            pl.BlockSpec((None, FFN_HALO, d), lambda bi, i: (bi, jnp.maximum(i * per - 1, 0), 0)),
            pl.BlockSpec((None, 6, d), lambda bi, i: (bi, 0, 0)),
            pl.BlockSpec((1, d), lambda bi, i: (0, 0)),
            _resident(wg.shape),
            _resident(wv.shape),
            pl.BlockSpec(conv_w.shape, lambda bi, i: (0, 0)),
            pl.BlockSpec(conv_b.shape, lambda bi, i: (0, 0)),
        ],
        out_specs=pl.BlockSpec((None, tm, f), lambda bi, i: (bi, i, 0)),
        out_shape=jax.ShapeDtypeStruct((b, t, f), BF16),
        scratch_shapes=[pltpu.VMEM((tm + FFN_HALO, d), BF16), pltpu.VMEM((tm + FFN_HALO, chunk), F32)],
        compiler_params=_params(("parallel", "parallel")),
        name="ffn_up_conv_gate",
    )(x, x, mod, gain, wg, wv, conv_w, conv_b)


def _lane_row(values, offset):
    return jnp.zeros((1, LANES), F32).at[0, offset:offset + values.shape[0]].set(values)


def kernel(x, c, ada_w, ada_b, norm_mix, norm_ffn, ev_w_in, ev_w_out, gdn_conv_w, gdn_a_log,
           gdn_dt_bias, gdn_norm, pool_w, pool_scale, od_w_in, od_w_out, att_q_norm, att_k_norm,
           ffn_w_up, ffn_conv_w, ffn_conv_b, ffn_w_down):
    b, t, d = x.shape
    depth = ada_w.shape[0]
    assert t % (ATT_BLOCK * max(dl for _, dl in DIL_PATTERNS)) == 0 and t % ROW_TILE == 0
    d_ff = ffn_w_down.shape[1]

    mod = _ada(c, ada_w, ada_b)

    i1 = GDN_QKV_W + GDN_V_W
    i2 = i1 + 2 * GDN_HEADS
    pad = jnp.zeros(ev_w_in.shape[:2] + (LANES - 2 * GDN_HEADS,), ev_w_in.dtype)
    ev_w = jnp.concatenate([ev_w_in[..., :i1], ev_w_in[..., i2:], ev_w_in[..., i1:i2], pad],
                           axis=-1).astype(BF16)

    for i in range(depth):
        m = mod[i]
        gain_m = norm_mix[i].reshape(1, d)
        if i % 2 == 0:
            e = i // 2
            qkv, z, p, ba = _even_in(x, m, gain_m, ev_w[e])
            oab = _gdn_pool(
                qkv, z, p, ba, gdn_conv_w[e],
                _lane_row(gdn_a_log[e], GDN_HEADS), _lane_row(gdn_dt_bias[e], GDN_HEADS),
                gdn_norm[e].reshape(1, GDN_DV), pool_w[e].astype(BF16),
                pool_scale[e].reshape(1, POOL_W))
            x = _out_res(oab, ev_w_out[e].astype(BF16), x, m, 2)
        else:
            o = i // 2
            qkv = _odd_in(x, m, gain_m, od_w_in[o].astype(BF16),
                          att_q_norm[o].reshape(1, ATT_DH), att_k_norm[o].reshape(1, ATT_DH))
            outs, lses = [], []
            for gi, (window, dil) in enumerate(DIL_PATTERNS):
                og, lg = _attention_group(qkv, gi, window, dil)
                outs.append(og)
                lses.append(lg)
            x = _merge_out(outs, lses, od_w_out[o].astype(BF16), x, m)

        w_up = ffn_w_up[i].astype(BF16)
        act = _ffn_up(x, m, norm_ffn[i].reshape(1, d), w_up[:, :d_ff], w_up[:, d_ff:],
                      ffn_conv_w[i], ffn_conv_b[i].reshape(1, d_ff))
        x = _out_res(act, ffn_w_down[i].astype(BF16), x, m, 5)
    return x
```

```python
import functools

import jax
import jax.numpy as jnp
from jax import lax
from jax.experimental import pallas as pl
from jax.experimental.pallas import tpu as pltpu

F32 = jnp.float32
BF16 = jnp.bfloat16

RMS_EPS = 1e-6

GDN_HEADS = 4
GDN_DK = 128
GDN_DV = 128
GDN_QK_W = GDN_HEADS * GDN_DK
GDN_V_W = GDN_HEADS * GDN_DV
GDN_QKV_W = 2 * GDN_QK_W + GDN_V_W
GDN_CONV = 4
GDN_CHUNK = 64
GDN_PAIR = 2 * GDN_CHUNK
GDN_TILE = 2 * GDN_PAIR
GDN_INV_BLOCK = 16

POOL_WINDOWS = (2, 4, 8, 16)
POOL_GROUP_W = 128
POOL_W = len(POOL_WINDOWS) * POOL_GROUP_W
POOL_HALO = 16

DIL_PATTERNS = ((128, 1), (512, 4), (2048, 16))
ATT_HEADS = 8
ATT_DH = 128
ATT_W = ATT_HEADS * ATT_DH
ATT_BLOCK = 128
ATT_SLOPES = tuple(2.0 ** (-8.0 * (h + 1) / ATT_HEADS) for h in range(ATT_HEADS))
MASKED_SCORE = -1e30

FFN_CONV = 3
FFN_HALO = 16
LANES = 128
ROW_TILE = 512
VMEM_LIMIT = 56 * 1024 * 1024


def _sigmoid(x):
    return 1.0 / (1.0 + jnp.exp(-x))


def _silu(x):
    return x * _sigmoid(x)


def _softplus(x):
    return jnp.maximum(x, 0.0) + jnp.log(1.0 + jnp.exp(-jnp.abs(x)))


def _mm(a, b):
    return jnp.dot(a, b, preferred_element_type=F32)


def _mm_nt(a, b):
    return lax.dot_general(a, b, (((1,), (1,)), ((), ())), preferred_element_type=F32)


def _mod_norm(x, gain, shift, scale):
    ms = jnp.mean(x * x, axis=-1, keepdims=True)
    return x * lax.rsqrt(ms + RMS_EPS) * (gain * (1.0 + scale)) + shift


def _params(semantics):
    return pltpu.CompilerParams(dimension_semantics=semantics, vmem_limit_bytes=VMEM_LIMIT)


def _resident(shape):
    nd = len(shape)
    return pl.BlockSpec(shape, lambda *_: (0,) * nd, pipeline_mode=pl.Buffered(1))


def _ada_kernel(c_ref, w_ref, b_ref, o_ref):
    cs = _silu(c_ref[...])
    o_ref[...] = _mm(cs.astype(BF16), w_ref[...].astype(BF16)) + b_ref[...]


def _ada(c, ada_w, ada_b):
    depth, d, n = ada_w.shape
    b = c.shape[0]
    rows = 8
    cp = jnp.pad(c, ((0, rows - b), (0, 0)))
    tn = 1536
    out = pl.pallas_call(
        _ada_kernel,
        grid=(depth, n // tn),
        in_specs=[
            pl.BlockSpec((rows, d), lambda l, j: (0, 0)),
            pl.BlockSpec((None, d, tn), lambda l, j: (l, 0, j)),
            pl.BlockSpec((None, 1, tn), lambda l, j: (l, 0, j)),
        ],
        out_specs=pl.BlockSpec((None, rows, tn), lambda l, j: (l, 0, j)),
        out_shape=jax.ShapeDtypeStruct((depth, rows, n), F32),
        compiler_params=_params(("parallel", "parallel")),
        name="ada_mod",
    )(cp, ada_w, ada_b.reshape(depth, 1, n))
    return out[:, :b].reshape(depth, b, 6, d)


def _even_in_kernel(x_ref, mod_ref, gain_ref, w_ref, qkv_ref, z_ref, p_ref, ba_ref, h_scr):
    m = mod_ref[...]
    h_scr[...] = _mod_norm(x_ref[...], gain_ref[...], m[0:1], m[1:2]).astype(BF16)
    col = 0
    for ref in (qkv_ref, z_ref, p_ref, ba_ref):
        width = ref.shape[-1]
        for lo in range(0, width, 512):
            hi = min(lo + 512, width)
            ref[:, lo:hi] = _mm(h_scr[...], w_ref[:, col + lo:col + hi]).astype(ref.dtype)
        col += width


def _even_in(x, mod, gain, w):
    b, t, d = x.shape
    tm = ROW_TILE
    row = lambda width: pl.BlockSpec((None, tm, width), lambda bi, i: (bi, i, 0))
    return pl.pallas_call(
        _even_in_kernel,
        grid=(b, t // tm),
        in_specs=[
            row(d),
            pl.BlockSpec((None, 6, d), lambda bi, i: (bi, 0, 0)),
            pl.BlockSpec((1, d), lambda bi, i: (0, 0)),
            _resident(w.shape),
        ],
        out_specs=[row(GDN_QKV_W), row(GDN_V_W), row(POOL_W), row(LANES)],
        out_shape=[
            jax.ShapeDtypeStruct((b, t, GDN_QKV_W), BF16),
            jax.ShapeDtypeStruct((b, t, GDN_V_W), BF16),
            jax.ShapeDtypeStruct((b, t, POOL_W), BF16),
            jax.ShapeDtypeStruct((b, t, LANES), F32),
        ],
        scratch_shapes=[pltpu.VMEM((tm, d), BF16)],
        compiler_params=_params(("parallel", "parallel")),
        name="even_in_proj",
    )(x, mod, gain, w)


def _lane_bcast(x, lane):
    return jnp.broadcast_to(x[:, lane:lane + 1], (x.shape[0], LANES))


def _row_bcast(x, row, rows):
    return jnp.broadcast_to(x[row:row + 1, :], (rows, x.shape[1]))


def _l2norm(x):
    return x * lax.rsqrt(jnp.sum(x * x, axis=-1, keepdims=True) + RMS_EPS)


def _unit_lower_solve(lows, rhss, blk, eye):
    bf = lambda xs: [x.astype(BF16) for x in xs]
    mm = lambda xs, ys: [_mm(x, y) for x, y in zip(xs, ys)]
    d = [jnp.where(blk, low, 0.0) for low in lows]
    n = bf([low - di for low, di in zip(lows, d)])
    db = bf(d)
    d2 = mm(db, db)
    d2b = bf(d2)
    d4 = mm(d2b, d2b)
    d4b = bf(d4)
    d8 = mm(d4b, d4b)
    acc = mm(bf([eye - x for x in d]), bf([eye + x for x in d2]))
    acc = mm(bf(acc), bf([eye + x for x in d4]))
    dinv = bf(mm(bf(acc), bf([eye + x for x in d8])))
    m = mm(dinv, n)
    mb = bf(m)
    m2 = mm(mb, mb)
    t1 = bf(mm(dinv, rhss))
    t2 = bf(mm(bf([eye + x for x in m2]), t1))
    return mm(bf([eye - x for x in m]), t2)


def _gdn_pool_kernel(qkv_ref, z_ref, p_ref, ba_ref, convw_ref, alog_ref, dtb_ref, gnorm_ref,
                     poolw_ref, pscale_ref, o_ref, qkv_ext, p_ext, s_scr):
    tt = GDN_TILE
    ch = GDN_CHUNK
    t = pl.program_id(1)

    @pl.when(t == 0)
    def _():
        qkv_ext[0:8, :] = jnp.zeros((8, GDN_QKV_W), F32)
        p_ext[0:POOL_HALO, :] = jnp.zeros((POOL_HALO, POOL_W), F32)
        s_scr[...] = jnp.zeros(s_scr.shape, F32)

    qkv_ext[8:8 + tt, :] = qkv_ref[...].astype(F32)
    p_ext[POOL_HALO:POOL_HALO + tt, :] = p_ref[...].astype(F32)

    cw = convw_ref[...]
    acc = cw[GDN_CONV - 1:GDN_CONV] * qkv_ext[8:8 + tt, :]
    for j in range(GDN_CONV - 1):
        acc = acc + cw[j:j + 1] * qkv_ext[pl.ds(8 - (GDN_CONV - 1) + j, tt), :]
    qkv = _silu(acc)

    pt = GDN_PAIR
    row = lax.broadcasted_iota(jnp.int32, (pt, pt), 0)
    col = lax.broadcasted_iota(jnp.int32, (pt, pt), 1)
    chunk_of = lambda idx: jnp.right_shift(idx, ch.bit_length() - 1)
    block_of = lambda idx: jnp.right_shift(idx, GDN_INV_BLOCK.bit_length() - 1)
    same = chunk_of(row) == chunk_of(col)
    causal = jnp.logical_and(same, row >= col)
    strict = jnp.logical_and(same, row > col)
    blk = block_of(row) == block_of(col)
    eye = jnp.where(row == col, 1.0, 0.0).astype(F32)
    first = row < ch
    tri = jnp.where(causal, 1.0, 0.0).astype(F32)

    pairs = tt // pt
    units = [(p, h) for p in range(pairs) for h in range(GDN_HEADS)]
    rows = lambda p: slice(p * pt, (p + 1) * pt)
    per_unit = lambda fn: [fn(p, h) for p, h in units]
    bf = lambda xs: [x.astype(BF16) for x in xs]

    beta_all, gc_all, gc_all_t = [], [], []
    for p in range(pairs):
        ba = ba_ref[rows(p), :]
        beta_all.append(_sigmoid(ba))
        g = -jnp.exp(alog_ref[...]) * _softplus(ba + dtb_ref[...])
        gc = jnp.dot(tri, g, preferred_element_type=F32, precision=lax.Precision.HIGHEST)
        gc_all.append(gc)
        gc_all_t.append(gc.T)

    q = per_unit(lambda p, h: _l2norm(qkv[rows(p), h * GDN_DK:(h + 1) * GDN_DK]) * (GDN_DK ** -0.5))
    k = per_unit(lambda p, h: _l2norm(
        qkv[rows(p), GDN_QK_W + h * GDN_DK:GDN_QK_W + (h + 1) * GDN_DK]))
    v = per_unit(lambda p, h: qkv[rows(p), 2 * GDN_QK_W + h * GDN_DV:2 * GDN_QK_W + (h + 1) * GDN_DV])
    beta = per_unit(lambda p, h: _lane_bcast(beta_all[p], h))
    gcol = per_unit(lambda p, h: _lane_bcast(gc_all[p], GDN_HEADS + h))
    grow = per_unit(lambda p, h: _row_bcast(gc_all_t[p], GDN_HEADS + h, pt))
    decay = [jnp.where(causal, jnp.exp(jnp.where(causal, gi - gj, 0.0)), 0.0)
             for gi, gj in zip(gcol, grow)]
    egc = [jnp.exp(g) for g in gcol]
    kb = [ki * bi for ki, bi in zip(k, beta)]
    k16 = bf(k)
    low = [jnp.where(strict, _mm_nt(a, b) * d, 0.0) for a, b, d in zip(bf(kb), k16, decay)]
    intra = bf([_mm_nt(a, b) * d for a, b, d in zip(bf(q), k16, decay)])
    rhs = bf([jnp.concatenate([vi * bi, kbi * ei], axis=1)
              for vi, bi, kbi, ei in zip(v, beta, kb, egc)])
    uw = bf(_unit_lower_solve(low, rhs, blk, eye))

    iu = [_mm(a, b) for a, b in zip(intra, uw)]
    o_const = [x[:, :GDN_DV] for x in iu]
    q_eff = bf([qi * ei - x[:, GDN_DV:] for qi, ei, x in zip(q, egc, iu)])
    glast = [jnp.where(first, _row_bcast(g, ch - 1, pt), _row_bcast(g, pt - 1, pt)) for g in gcol]
    kdec = [ki * jnp.exp(gl - g) for ki, gl, g in zip(k, glast, gcol)]
    chunks = [slice(c * ch, (c + 1) * ch) for c in range(pt // ch)]
    ku = [[_mm(kd[rs].T.astype(BF16), x[rs]) for rs in chunks] for kd, x in zip(kdec, uw)]

    s = [s_scr[h] for h in range(GDN_HEADS)]
    outs = {}
    for p in range(pairs):
        for c, rs in enumerate(chunks):
            for h in range(GDN_HEADS):
                i = p * GDN_HEADS + h
                sb = s[h].astype(BF16)
                outs[p, h, c] = _mm(q_eff[i][rs], sb) + o_const[i][rs]
                gl = _row_bcast(egc[i], rs.stop - 1, GDN_DK)
                s[h] = gl * s[h] + ku[i][c][:, :GDN_DV] - _mm(ku[i][c][:, GDN_DV:].astype(BF16), sb)
    for h in range(GDN_HEADS):
        s_scr[h] = s[h]

    gnorm = gnorm_ref[...]
    for p, h in units:
        hs = slice(h * GDN_DV, (h + 1) * GDN_DV)
        o = jnp.concatenate([outs[p, h, c] for c in range(len(chunks))], axis=0)
        o = o * lax.rsqrt(jnp.mean(o * o, axis=-1, keepdims=True) + RMS_EPS) * gnorm
        o_ref[rows(p), hs] = (o * _silu(z_ref[rows(p), hs].astype(F32))).astype(o_ref.dtype)

    tok = (t * tt + 1 + lax.broadcasted_iota(jnp.int32, (tt, POOL_GROUP_W), 0)).astype(F32)
    pscale = pscale_ref[...]
    for gi, win in enumerate(POOL_WINDOWS):
        cs = slice(gi * POOL_GROUP_W, (gi + 1) * POOL_GROUP_W)
        cur = p_ext[POOL_HALO:POOL_HALO + tt, cs]
        tot = cur
        for back in range(1, win):
            tot = tot + p_ext[pl.ds(POOL_HALO - back, tt), cs]
        pooled = tot / jnp.minimum(tok, float(win)) - cur
        y = _mm(pooled.astype(BF16), poolw_ref[gi])
        o_ref[:, GDN_V_W + gi * POOL_GROUP_W:GDN_V_W + (gi + 1) * POOL_GROUP_W] = (
            y * pscale[:, cs]).astype(o_ref.dtype)

    qkv_ext[0:8, :] = qkv_ext[tt:tt + 8, :]
    p_ext[0:POOL_HALO, :] = p_ext[tt:tt + POOL_HALO, :]


def _gdn_pool(qkv, z, p, ba, conv_w, alog_row, dtb_row, gnorm, pool_w, pool_scale):
    b, t, _ = qkv.shape
    tt = GDN_TILE
    row = lambda width: pl.BlockSpec((None, tt, width), lambda bi, i: (bi, i, 0))
    full = lambda a: pl.BlockSpec(a.shape, lambda bi, i: (0,) * a.ndim)
    small = (conv_w, alog_row, dtb_row, gnorm, pool_w, pool_scale)
    return pl.pallas_call(
        _gdn_pool_kernel,
        grid=(b, t // tt),
        in_specs=[row(GDN_QKV_W), row(GDN_V_W), row(POOL_W), row(LANES)] + [full(a) for a in small],
        out_specs=row(GDN_V_W + POOL_W),
        out_shape=jax.ShapeDtypeStruct((b, t, GDN_V_W + POOL_W), BF16),
        scratch_shapes=[
            pltpu.VMEM((tt + 8, GDN_QKV_W), F32),
            pltpu.VMEM((tt + POOL_HALO, POOL_W), F32),
            pltpu.VMEM((GDN_HEADS, GDN_DK, GDN_DV), F32),
        ],
        compiler_params=_params(("parallel", "arbitrary")),
        name="gdn_pool",
    )(qkv, z, p, ba, *small)


def _out_res_kernel(a_ref, w_ref, x_ref, mod_ref, o_ref, *, gate_row):
    gate = mod_ref[...][gate_row:gate_row + 1]
    o_ref[...] = x_ref[...] + gate * _mm(a_ref[...], w_ref[...])


def _out_res(a, w, x, mod, gate_row):
    b, t, d = x.shape
    k = a.shape[-1]
    tm = ROW_TILE
    return pl.pallas_call(
        functools.partial(_out_res_kernel, gate_row=gate_row),
        grid=(b, t // tm),
        in_specs=[
            pl.BlockSpec((None, tm, k), lambda bi, i: (bi, i, 0)),
            _resident(w.shape),
            pl.BlockSpec((None, tm, d), lambda bi, i: (bi, i, 0)),
            pl.BlockSpec((None, 6, d), lambda bi, i: (bi, 0, 0)),
        ],
        out_specs=pl.BlockSpec((None, tm, d), lambda bi, i: (bi, i, 0)),
        out_shape=jax.ShapeDtypeStruct((b, t, d), F32),
        compiler_params=_params(("parallel", "parallel")),
        name="out_proj_residual",
    )(a, w, x, mod)


def _odd_in_kernel(x_ref, mod_ref, gain_ref, w_ref, qn_ref, kn_ref, o0_ref, o1_ref, o2_ref,
                   h_scr, slab):
    tm = x_ref.shape[0]
    m = mod_ref[...]
    h_scr[...] = _mod_norm(x_ref[...], gain_ref[...], m[0:1], m[1:2]).astype(BF16)
    qn = qn_ref[...] * (ATT_DH ** -0.5)
    kn = kn_ref[...]
    half = ATT_W // 2
    for gi, o_ref in enumerate((o0_ref, o1_ref, o2_ref)):
        dil = DIL_PATTERNS[gi][1]
        for blk in range(3 * ATT_W // half):
            lo = blk * half
            y = _mm(h_scr[...], w_ref[:, gi * 3 * ATT_W + lo:gi * 3 * ATT_W + lo + half])
            part = lo // ATT_W
            for h in range(half // ATT_DH):
                cs = slice(lo + h * ATT_DH, lo + (h + 1) * ATT_DH)
                yh = y[:, h * ATT_DH:(h + 1) * ATT_DH]
                if part < 2:
                    ms = jnp.mean(yh * yh, axis=-1, keepdims=True)
                    yh = yh * lax.rsqrt(ms + RMS_EPS) * (qn if part == 0 else kn)
                if dil == 1:
                    o_ref[0, :, cs] = yh.astype(o_ref.dtype)
                    continue
                slab[...] = yh
                for r in range(dil):
                    o_ref[r, :, cs] = slab[pl.ds(r, tm // dil, stride=dil), :].astype(o_ref.dtype)


def _odd_in(x, mod, gain, w, q_norm, k_norm):
    b, t, d = x.shape
    tm = ROW_TILE
    dils = [dl for _, dl in DIL_PATTERNS]
    return pl.pallas_call(
        _odd_in_kernel,
        grid=(b, t // tm),
        in_specs=[
            pl.BlockSpec((None, tm, d), lambda bi, i: (bi, i, 0)),
            pl.BlockSpec((None, 6, d), lambda bi, i: (bi, 0, 0)),
            pl.BlockSpec((1, d), lambda bi, i: (0, 0)),
            _resident(w.shape),
            pl.BlockSpec((1, ATT_DH), lambda bi, i: (0, 0)),
            pl.BlockSpec((1, ATT_DH), lambda bi, i: (0, 0)),
        ],
        out_specs=[pl.BlockSpec((None, dl, tm // dl, 3 * ATT_W), lambda bi, i: (bi, 0, i, 0))
                   for dl in dils],
        out_shape=[jax.ShapeDtypeStruct((b, dl, t // dl, 3 * ATT_W), BF16) for dl in dils],
        scratch_shapes=[pltpu.VMEM((tm, d), BF16), pltpu.VMEM((tm, ATT_DH), F32)],
        compiler_params=_params(("parallel", "parallel")),
        name="odd_in_proj",
    )(x, mod, gain, w, q_norm, k_norm)


def _attn_kernel(q_ref, kp_ref, kc_ref, vp_ref, vc_ref, o_ref, lse_ref, bias_scr, *, dil, n_back):
    blk = ATT_BLOCK
    first_step = jnp.logical_and(jnp.logical_and(pl.program_id(0) == 0, pl.program_id(1) == 0),
                                 pl.program_id(2) == 0)

    @pl.when(first_step)
    def _():
        a = lax.broadcasted_iota(jnp.int32, (blk, 2 * blk), 0)
        j = lax.broadcasted_iota(jnp.int32, (blk, 2 * blk), 1)
        rel = blk + a - j
        in_band = jnp.logical_and(rel >= 0, rel <= n_back)
        own = jnp.logical_and(in_band, j >= blk)
        relf = rel.astype(F32)
        for h in range(ATT_HEADS):
            bias = -(ATT_SLOPES[h] * dil) * relf
            bias_scr[0, h] = jnp.where(own, bias, MASKED_SCORE)
            bias_scr[1, h] = jnp.where(in_band, bias, MASKED_SCORE)

    variant = jnp.minimum(pl.program_id(2), 1)
    heads = [slice(h * ATT_DH, (h + 1) * ATT_DH) for h in range(ATT_HEADS)]
    s = [_mm_nt(q_ref[:, hs], jnp.concatenate([kp_ref[:, hs], kc_ref[:, hs]], axis=0))
         + bias_scr[variant, h] for h, hs in enumerate(heads)]
    m = [jnp.max(sh, axis=-1, keepdims=True) for sh in s]
    p = [jnp.exp(sh - mh) for sh, mh in zip(s, m)]
    l = [jnp.sum(ph, axis=-1, keepdims=True) for ph in p]
    o = [_mm(ph.astype(BF16), jnp.concatenate([vp_ref[:, hs], vc_ref[:, hs]], axis=0))
         for ph, hs in zip(p, heads)]
    lane = lax.broadcasted_iota(jnp.int32, (blk, LANES), 1)
    lse_all = jnp.zeros((blk, LANES), F32)
    for h, hs in enumerate(heads):
        o_ref[:, hs] = (o[h] * (1.0 / l[h])).astype(o_ref.dtype)
        lse_all = jnp.where(lane == h, m[h] + jnp.log(l[h]), lse_all)
    lse_ref[...] = lse_all


def _attention_group(qkv, window, dil):
    b, _, length, _ = qkv.shape
    nb = length // ATT_BLOCK

    def spec(part, prev):
        def index(bi, r, n):
            return (bi, r, jnp.maximum(n - 1, 0) if prev else n, part)
        return pl.BlockSpec((None, None, ATT_BLOCK, ATT_W), index)

    return pl.pallas_call(
        functools.partial(_attn_kernel, dil=dil, n_back=window // dil),
        grid=(b, dil, nb),
        in_specs=[spec(0, False), spec(1, True), spec(1, False), spec(2, True), spec(2, False)],
        out_specs=[
            pl.BlockSpec((None, None, ATT_BLOCK, ATT_W), lambda bi, r, n: (bi, r, n, 0)),
            pl.BlockSpec((None, None, ATT_BLOCK, LANES), lambda bi, r, n: (bi, r, n, 0)),
        ],
        out_shape=[
            jax.ShapeDtypeStruct((b, dil, length, ATT_W), BF16),
            jax.ShapeDtypeStruct((b, dil, length, LANES), F32),
        ],
        scratch_shapes=[pltpu.VMEM((2, ATT_HEADS, ATT_BLOCK, 2 * ATT_BLOCK), F32)],
        compiler_params=_params(("arbitrary", "arbitrary", "arbitrary")),
        name=f"dilated_attention_d{dil}",
    )(qkv, qkv, qkv, qkv, qkv)


def _merge_out_kernel(o0_ref, o1_ref, o2_ref, l0_ref, l1_ref, l2_ref, w_ref, x_ref, mod_ref,
                      out_ref, o_slab, l_slab, a_scr):
    tm = x_ref.shape[0]
    o_refs = (o0_ref, o1_ref, o2_ref)
    l_refs = (l0_ref, l1_ref, l2_ref)
    heads = [slice(h * ATT_DH, (h + 1) * ATT_DH) for h in range(ATT_HEADS)]
    for gi, (_, dil) in enumerate(DIL_PATTERNS):
        if dil == 1:
            continue
        for r in range(dil):
            dst = pl.ds(r, tm // dil, stride=dil)
            l_slab[gi, dst, :] = l_refs[gi][r]
            for h, hs in enumerate(heads):
                o_slab[gi, h, dst, :] = o_refs[gi][r, :, hs].astype(F32)

    def lse_of(gi):
        return l_refs[gi][0] if DIL_PATTERNS[gi][1] == 1 else l_slab[gi]

    def out_of(gi, h):
        if DIL_PATTERNS[gi][1] == 1:
            return o_refs[gi][0, :, heads[h]].astype(F32)
        return o_slab[gi, h]

    l0, l1, l2 = lse_of(0), lse_of(1), lse_of(2)
    mx = jnp.maximum(jnp.maximum(l0, l1), l2)
    e0, e1, e2 = jnp.exp(l0 - mx), jnp.exp(l1 - mx), jnp.exp(l2 - mx)
    inv = 1.0 / (e0 + e1 + e2)
    wts = (e0 * inv, e1 * inv, e2 * inv)
    for h, hs in enumerate(heads):
        merged = (_lane_bcast(wts[0], h) * out_of(0, h) + _lane_bcast(wts[1], h) * out_of(1, h)
                  + _lane_bcast(wts[2], h) * out_of(2, h))
        a_scr[:, hs] = merged.astype(BF16)
    gate = mod_ref[...][2:3]
    out_ref[...] = x_ref[...] + gate * _mm(a_scr[...], w_ref[...])


def _merge_out(outs, lses, w, x, mod):
    b, t, d = x.shape
    tm = ROW_TILE
    groups = len(DIL_PATTERNS)
    row = lambda width: pl.BlockSpec((None, tm, width), lambda bi, i: (bi, i, 0))
    by_residue = lambda dl, width: pl.BlockSpec((None, dl, tm // dl, width),
                                                lambda bi, i: (bi, 0, i, 0))
    dils = [dl for _, dl in DIL_PATTERNS]
    return pl.pallas_call(
        _merge_out_kernel,
        grid=(b, t // tm),
        in_specs=[by_residue(dl, ATT_W) for dl in dils] + [by_residue(dl, LANES) for dl in dils] + [
            _resident(w.shape), row(d), pl.BlockSpec((None, 6, d), lambda bi, i: (bi, 0, 0))],
        out_specs=row(d),
        out_shape=jax.ShapeDtypeStruct((b, t, d), F32),
        scratch_shapes=[
            pltpu.VMEM((groups, ATT_HEADS, tm, ATT_DH), F32),
            pltpu.VMEM((groups, tm, LANES), F32),
            pltpu.VMEM((tm, ATT_W), BF16),
        ],
        compiler_params=_params(("parallel", "parallel")),
        name="attn_merge_out_proj",
    )(*outs, *lses, w, x, mod)


def _ffn_up_kernel(x_ref, xh_ref, mod_ref, gain_ref, wg_ref, wv_ref, cw_ref, cb_ref, o_ref,
                   h_scr, g_scr, *, chunk):
    tm = x_ref.shape[0]
    halo = FFN_HALO
    m = mod_ref[...]
    gain = gain_ref[...]
    shift, scale = m[3:4], m[4:5]
    h_scr[halo:, :] = _mod_norm(x_ref[...], gain, shift, scale).astype(BF16)
    live = jnp.where(pl.program_id(1) > 0, 1.0, 0.0)
    h_scr[0:halo, :] = (_mod_norm(xh_ref[...], gain, shift, scale) * live).astype(BF16)
    cw = cw_ref[...]
    cb = cb_ref[...]
    for lo in range(0, o_ref.shape[-1], chunk):
        cs = slice(lo, lo + chunk)
        g_scr[...] = _mm(h_scr[...], wg_ref[:, cs])
        val = _mm(h_scr[halo:, :], wv_ref[:, cs])
        gate = cb[:, cs] + cw[FFN_CONV - 1:FFN_CONV, cs] * g_scr[halo:, :]
        for j in range(FFN_CONV - 1):
            gate = gate + cw[j:j + 1, cs] * g_scr[pl.ds(halo - (FFN_CONV - 1) + j, tm), :]
        o_ref[:, cs] = (_silu(gate) * val).astype(o_ref.dtype)


def _ffn_up(x, mod, gain, wg, wv, conv_w, conv_b):
    b, t, d = x.shape
    f = wg.shape[-1]
    tm = ROW_TILE
    chunk = 256
    per = tm // FFN_HALO
    return pl.pallas_call(
        functools.partial(_ffn_up_kernel, chunk=chunk),
        grid=(b, t // tm),
        in_specs=[
            pl.BlockSpec((None, tm, d), lambda bi, i: (bi, i, 0)),
            pl.BlockSpec((None, FFN_HALO, d), lambda bi, i: (bi, jnp.maximum(i * per - 1, 0), 0)),
            pl.BlockSpec((None, 6, d), lambda bi, i: (bi, 0, 0)),
            pl.BlockSpec((1, d), lambda bi, i: (0, 0)),
            _resident(wg.shape),
            _resident(wv.shape),
            pl.BlockSpec(conv_w.shape, lambda bi, i: (0, 0)),
            pl.BlockSpec(conv_b.shape, lambda bi, i: (0, 0)),
        ],
        out_specs=pl.BlockSpec((None, tm, f), lambda bi, i: (bi, i, 0)),
        out_shape=jax.ShapeDtypeStruct((b, t, f), BF16),
        scratch_shapes=[pltpu.VMEM((tm + FFN_HALO, d), BF16), pltpu.VMEM((tm + FFN_HALO, chunk), F32)],
        compiler_params=_params(("parallel", "parallel")),
        name="ffn_up_conv_gate",
    )(x, x, mod, gain, wg, wv, conv_w, conv_b)


def _lane_row(values, offset):
    return jnp.zeros((1, LANES), F32).at[0, offset:offset + values.shape[0]].set(values)


def kernel(x, c, ada_w, ada_b, norm_mix, norm_ffn, ev_w_in, ev_w_out, gdn_conv_w, gdn_a_log,
           gdn_dt_bias, gdn_norm, pool_w, pool_scale, od_w_in, od_w_out, att_q_norm, att_k_norm,
           ffn_w_up, ffn_conv_w, ffn_conv_b, ffn_w_down):
    b, t, d = x.shape
    depth = ada_w.shape[0]
    assert t % (ATT_BLOCK * max(dl for _, dl in DIL_PATTERNS)) == 0 and t % ROW_TILE == 0
    d_ff = ffn_w_down.shape[1]

    mod = _ada(c, ada_w, ada_b)

    i1 = GDN_QKV_W + GDN_V_W
    i2 = i1 + 2 * GDN_HEADS
    pad = jnp.zeros(ev_w_in.shape[:2] + (LANES - 2 * GDN_HEADS,), ev_w_in.dtype)
    ev_w = jnp.concatenate([ev_w_in[..., :i1], ev_w_in[..., i2:], ev_w_in[..., i1:i2], pad],
                           axis=-1).astype(BF16)

    for i in range(depth):
        m = mod[i]
        gain_m = norm_mix[i].reshape(1, d)
        if i % 2 == 0:
            e = i // 2
            qkv, z, p, ba = _even_in(x, m, gain_m, ev_w[e])
            oab = _gdn_pool(
                qkv, z, p, ba, gdn_conv_w[e],
                _lane_row(gdn_a_log[e], GDN_HEADS), _lane_row(gdn_dt_bias[e], GDN_HEADS),
                gdn_norm[e].reshape(1, GDN_DV), pool_w[e].astype(BF16),
                pool_scale[e].reshape(1, POOL_W))
            x = _out_res(oab, ev_w_out[e].astype(BF16), x, m, 2)
        else:
            o = i // 2
            qkvs = _odd_in(x, m, gain_m, od_w_in[o].astype(BF16),
                           att_q_norm[o].reshape(1, ATT_DH), att_k_norm[o].reshape(1, ATT_DH))
            outs, lses = [], []
            for qkv, (window, dil) in zip(qkvs, DIL_PATTERNS):
                og, lg = _attention_group(qkv, window, dil)
                outs.append(og)
                lses.append(lg)
            x = _merge_out(outs, lses, od_w_out[o].astype(BF16), x, m)

        w_up = ffn_w_up[i].astype(BF16)
        act = _ffn_up(x, m, norm_ffn[i].reshape(1, d), w_up[:, :d_ff], w_up[:, d_ff:],
                      ffn_conv_w[i], ffn_conv_b[i].reshape(1, d_ff))
        x = _out_res(act, ffn_w_down[i].astype(BF16), x, m, 5)
    return x
```

```python
import functools

import jax
import jax.numpy as jnp
from jax import lax
from jax.experimental import pallas as pl
from jax.experimental.pallas import tpu as pltpu

F32 = jnp.float32
BF16 = jnp.bfloat16

RMS_EPS = 1e-6

GDN_HEADS = 4
GDN_DK = 128
GDN_DV = 128
GDN_QK_W = GDN_HEADS * GDN_DK
GDN_V_W = GDN_HEADS * GDN_DV
GDN_QKV_W = 2 * GDN_QK_W + GDN_V_W
GDN_CONV = 4
GDN_CHUNK = 64
GDN_PAIR = 2 * GDN_CHUNK
GDN_TILE = 2 * GDN_PAIR
GDN_INV_BLOCK = 16

POOL_WINDOWS = (2, 4, 8, 16)
POOL_GROUP_W = 128
POOL_W = len(POOL_WINDOWS) * POOL_GROUP_W
POOL_HALO = 16

DIL_PATTERNS = ((128, 1), (512, 4), (2048, 16))
ATT_HEADS = 8
ATT_DH = 128
ATT_W = ATT_HEADS * ATT_DH
ATT_BLOCK = 128
ATT_QBLOCKS = 4
ATT_SLOPES = tuple(2.0 ** (-8.0 * (h + 1) / ATT_HEADS) for h in range(ATT_HEADS))
MASKED_SCORE = -1e30
LOG2_E = 1.4426950408889634
LN_2 = 0.6931471805599453

FFN_CONV = 3
FFN_HALO = 16
LANES = 128
ROW_TILE = 512
VMEM_LIMIT = 56 * 1024 * 1024


def _sigmoid(x):
    return 1.0 / (1.0 + jnp.exp(-x))


def _silu(x):
    return x * _sigmoid(x)


def _softplus(x):
    return jnp.maximum(x, 0.0) + jnp.log(1.0 + jnp.exp(-jnp.abs(x)))


def _mm(a, b):
    return jnp.dot(a, b, preferred_element_type=F32)


def _mm_nt(a, b):
    return lax.dot_general(a, b, (((1,), (1,)), ((), ())), preferred_element_type=F32)


def _mod_norm(x, gain, shift, scale):
    ms = jnp.mean(x * x, axis=-1, keepdims=True)
    return x * lax.rsqrt(ms + RMS_EPS) * (gain * (1.0 + scale)) + shift


def _params(semantics):
    return pltpu.CompilerParams(dimension_semantics=semantics, vmem_limit_bytes=VMEM_LIMIT)


def _resident(shape):
    nd = len(shape)
    return pl.BlockSpec(shape, lambda *_: (0,) * nd, pipeline_mode=pl.Buffered(1))


def _ada_kernel(c_ref, w_ref, b_ref, o_ref):
    cs = _silu(c_ref[...])
    o_ref[...] = _mm(cs.astype(BF16), w_ref[...].astype(BF16)) + b_ref[...]


def _ada(c, ada_w, ada_b):
    depth, d, n = ada_w.shape
    b = c.shape[0]
    rows = 8
    cp = jnp.pad(c, ((0, rows - b), (0, 0)))
    tn = 1536
    out = pl.pallas_call(
        _ada_kernel,
        grid=(depth, n // tn),
        in_specs=[
            pl.BlockSpec((rows, d), lambda l, j: (0, 0)),
            pl.BlockSpec((None, d, tn), lambda l, j: (l, 0, j)),
            pl.BlockSpec((None, 1, tn), lambda l, j: (l, 0, j)),
        ],
        out_specs=pl.BlockSpec((None, rows, tn), lambda l, j: (l, 0, j)),
        out_shape=jax.ShapeDtypeStruct((depth, rows, n), F32),
        compiler_params=_params(("parallel", "parallel")),
        name="ada_mod",
    )(cp, ada_w, ada_b.reshape(depth, 1, n))
    return out[:, :b].reshape(depth, b, 6, d)


def _even_in_kernel(x_ref, mod_ref, gain_ref, w_ref, qkv_ref, z_ref, p_ref, ba_ref, h_scr):
    m = mod_ref[...]
    h_scr[...] = _mod_norm(x_ref[...], gain_ref[...], m[0:1], m[1:2]).astype(BF16)
    col = 0
    for ref in (qkv_ref, z_ref, p_ref, ba_ref):
        width = ref.shape[-1]
        for lo in range(0, width, 512):
            hi = min(lo + 512, width)
            ref[:, lo:hi] = _mm(h_scr[...], w_ref[:, col + lo:col + hi]).astype(ref.dtype)
        col += width


def _even_in(x, mod, gain, w):
    b, t, d = x.shape
    tm = ROW_TILE
    row = lambda width: pl.BlockSpec((None, tm, width), lambda bi, i: (bi, i, 0))
    return pl.pallas_call(
        _even_in_kernel,
        grid=(b, t // tm),
        in_specs=[
            row(d),
            pl.BlockSpec((None, 6, d), lambda bi, i: (bi, 0, 0)),
            pl.BlockSpec((1, d), lambda bi, i: (0, 0)),
            _resident(w.shape),
        ],
        out_specs=[row(GDN_QKV_W), row(GDN_V_W), row(POOL_W), row(LANES)],
        out_shape=[
            jax.ShapeDtypeStruct((b, t, GDN_QKV_W), BF16),
            jax.ShapeDtypeStruct((b, t, GDN_V_W), BF16),
            jax.ShapeDtypeStruct((b, t, POOL_W), BF16),
            jax.ShapeDtypeStruct((b, t, LANES), F32),
        ],
        scratch_shapes=[pltpu.VMEM((tm, d), BF16)],
        compiler_params=_params(("parallel", "parallel")),
        name="even_in_proj",
    )(x, mod, gain, w)


def _lane_bcast(x, lane):
    return jnp.broadcast_to(x[:, lane:lane + 1], (x.shape[0], LANES))


def _row_bcast(x, row, rows):
    return jnp.broadcast_to(x[row:row + 1, :], (rows, x.shape[1]))


def _l2norm(x):
    return x * lax.rsqrt(jnp.sum(x * x, axis=-1, keepdims=True) + RMS_EPS)


def _unit_lower_solve(lows, rhss, blk, eye):
    bf = lambda xs: [x.astype(BF16) for x in xs]
    mm = lambda xs, ys: [_mm(x, y) for x, y in zip(xs, ys)]
    d = [jnp.where(blk, low, 0.0) for low in lows]
    n = bf([low - di for low, di in zip(lows, d)])
    db = bf(d)
    d2 = mm(db, db)
    d2b = bf(d2)
    d4 = mm(d2b, d2b)
    d4b = bf(d4)
    d8 = mm(d4b, d4b)
    acc = mm(bf([eye - x for x in d]), bf([eye + x for x in d2]))
    acc = mm(bf(acc), bf([eye + x for x in d4]))
    dinv = bf(mm(bf(acc), bf([eye + x for x in d8])))
    m = mm(dinv, n)
    mb = bf(m)
    m2 = mm(mb, mb)
    t1 = bf(mm(dinv, rhss))
    t2 = bf(mm(bf([eye + x for x in m2]), t1))
    return mm(bf([eye - x for x in m]), t2)


def _gdn_pool_kernel(qkv_ref, z_ref, p_ref, ba_ref, convw_ref, alog_ref, dtb_ref, gnorm_ref,
                     poolw_ref, pscale_ref, o_ref, qkv_ext, p_ext, s_scr):
    tt = GDN_TILE
    ch = GDN_CHUNK
    t = pl.program_id(1)

    @pl.when(t == 0)
    def _():
        qkv_ext[0:8, :] = jnp.zeros((8, GDN_QKV_W), F32)
        p_ext[0:POOL_HALO, :] = jnp.zeros((POOL_HALO, POOL_W), F32)
        s_scr[...] = jnp.zeros(s_scr.shape, F32)

    qkv_ext[8:8 + tt, :] = qkv_ref[...].astype(F32)
    p_ext[POOL_HALO:POOL_HALO + tt, :] = p_ref[...].astype(F32)

    cw = convw_ref[...]
    acc = cw[GDN_CONV - 1:GDN_CONV] * qkv_ext[8:8 + tt, :]
    for j in range(GDN_CONV - 1):
        acc = acc + cw[j:j + 1] * qkv_ext[pl.ds(8 - (GDN_CONV - 1) + j, tt), :]
    qkv = _silu(acc)

    pt = GDN_PAIR
    row = lax.broadcasted_iota(jnp.int32, (pt, pt), 0)
    col = lax.broadcasted_iota(jnp.int32, (pt, pt), 1)
    chunk_of = lambda idx: jnp.right_shift(idx, ch.bit_length() - 1)
    block_of = lambda idx: jnp.right_shift(idx, GDN_INV_BLOCK.bit_length() - 1)
    same = chunk_of(row) == chunk_of(col)
    causal = jnp.logical_and(same, row >= col)
    strict = jnp.logical_and(same, row > col)
    blk = block_of(row) == block_of(col)
    eye = jnp.where(row == col, 1.0, 0.0).astype(F32)
    first = row < ch
    tri = jnp.where(causal, 1.0, 0.0).astype(F32)

    pairs = tt // pt
    units = [(p, h) for p in range(pairs) for h in range(GDN_HEADS)]
    rows = lambda p: slice(p * pt, (p + 1) * pt)
    per_unit = lambda fn: [fn(p, h) for p, h in units]
    bf = lambda xs: [x.astype(BF16) for x in xs]

    beta_all, gc_all, gc_all_t = [], [], []
    for p in range(pairs):
        ba = ba_ref[rows(p), :]
        beta_all.append(_sigmoid(ba))
        g = -jnp.exp(alog_ref[...]) * _softplus(ba + dtb_ref[...])
        gc = jnp.dot(tri, g, preferred_element_type=F32, precision=lax.Precision.HIGHEST)
        gc_all.append(gc)
        gc_all_t.append(gc.T)

    q = per_unit(lambda p, h: _l2norm(qkv[rows(p), h * GDN_DK:(h + 1) * GDN_DK]) * (GDN_DK ** -0.5))
    k = per_unit(lambda p, h: _l2norm(
        qkv[rows(p), GDN_QK_W + h * GDN_DK:GDN_QK_W + (h + 1) * GDN_DK]))
    v = per_unit(lambda p, h: qkv[rows(p), 2 * GDN_QK_W + h * GDN_DV:2 * GDN_QK_W + (h + 1) * GDN_DV])
    beta = per_unit(lambda p, h: _lane_bcast(beta_all[p], h))
    gcol = per_unit(lambda p, h: _lane_bcast(gc_all[p], GDN_HEADS + h))
    grow = per_unit(lambda p, h: _row_bcast(gc_all_t[p], GDN_HEADS + h, pt))
    decay = [jnp.where(causal, jnp.exp(jnp.where(causal, gi - gj, 0.0)), 0.0)
             for gi, gj in zip(gcol, grow)]
    egc = [jnp.exp(g) for g in gcol]
    kb = [ki * bi for ki, bi in zip(k, beta)]
    k16 = bf(k)
    low = [jnp.where(strict, _mm_nt(a, b) * d, 0.0) for a, b, d in zip(bf(kb), k16, decay)]
    intra = bf([_mm_nt(a, b) * d for a, b, d in zip(bf(q), k16, decay)])
    rhs = bf([jnp.concatenate([vi * bi, kbi * ei], axis=1)
              for vi, bi, kbi, ei in zip(v, beta, kb, egc)])
    uw = bf(_unit_lower_solve(low, rhs, blk, eye))

    iu = [_mm(a, b) for a, b in zip(intra, uw)]
    o_const = [x[:, :GDN_DV] for x in iu]
    q_eff = bf([qi * ei - x[:, GDN_DV:] for qi, ei, x in zip(q, egc, iu)])
    glast = [jnp.where(first, _row_bcast(g, ch - 1, pt), _row_bcast(g, pt - 1, pt)) for g in gcol]
    kdec = [ki * jnp.exp(gl - g) for ki, gl, g in zip(k, glast, gcol)]
    chunks = [slice(c * ch, (c + 1) * ch) for c in range(pt // ch)]
    ku = [[_mm(kd[rs].T.astype(BF16), x[rs]) for rs in chunks] for kd, x in zip(kdec, uw)]

    s = [s_scr[h] for h in range(GDN_HEADS)]
    outs = {}
    for p in range(pairs):
        for c, rs in enumerate(chunks):
            for h in range(GDN_HEADS):
                i = p * GDN_HEADS + h
                sb = s[h].astype(BF16)
                outs[p, h, c] = _mm(q_eff[i][rs], sb) + o_const[i][rs]
                gl = _row_bcast(egc[i], rs.stop - 1, GDN_DK)
                s[h] = gl * s[h] + ku[i][c][:, :GDN_DV] - _mm(ku[i][c][:, GDN_DV:].astype(BF16), sb)
    for h in range(GDN_HEADS):
        s_scr[h] = s[h]

    gnorm = gnorm_ref[...]
    for p, h in units:
        hs = slice(h * GDN_DV, (h + 1) * GDN_DV)
        o = jnp.concatenate([outs[p, h, c] for c in range(len(chunks))], axis=0)
        o = o * lax.rsqrt(jnp.mean(o * o, axis=-1, keepdims=True) + RMS_EPS) * gnorm
        o_ref[rows(p), hs] = (o * _silu(z_ref[rows(p), hs].astype(F32))).astype(o_ref.dtype)

    tok = (t * tt + 1 + lax.broadcasted_iota(jnp.int32, (tt, POOL_GROUP_W), 0)).astype(F32)
    pscale = pscale_ref[...]
    for gi, win in enumerate(POOL_WINDOWS):
        cs = slice(gi * POOL_GROUP_W, (gi + 1) * POOL_GROUP_W)
        cur = p_ext[POOL_HALO:POOL_HALO + tt, cs]
        tot = cur
        for back in range(1, win):
            tot = tot + p_ext[pl.ds(POOL_HALO - back, tt), cs]
        pooled = tot / jnp.minimum(tok, float(win)) - cur
        y = _mm(pooled.astype(BF16), poolw_ref[gi])
        o_ref[:, GDN_V_W + gi * POOL_GROUP_W:GDN_V_W + (gi + 1) * POOL_GROUP_W] = (
            y * pscale[:, cs]).astype(o_ref.dtype)

    qkv_ext[0:8, :] = qkv_ext[tt:tt + 8, :]
    p_ext[0:POOL_HALO, :] = p_ext[tt:tt + POOL_HALO, :]


def _gdn_pool(qkv, z, p, ba, conv_w, alog_row, dtb_row, gnorm, pool_w, pool_scale):
    b, t, _ = qkv.shape
    tt = GDN_TILE
    row = lambda width: pl.BlockSpec((None, tt, width), lambda bi, i: (bi, i, 0))
    full = lambda a: pl.BlockSpec(a.shape, lambda bi, i: (0,) * a.ndim)
    small = (conv_w, alog_row, dtb_row, gnorm, pool_w, pool_scale)
    return pl.pallas_call(
        _gdn_pool_kernel,
        grid=(b, t // tt),
        in_specs=[row(GDN_QKV_W), row(GDN_V_W), row(POOL_W), row(LANES)] + [full(a) for a in small],
        out_specs=row(GDN_V_W + POOL_W),
        out_shape=jax.ShapeDtypeStruct((b, t, GDN_V_W + POOL_W), BF16),
        scratch_shapes=[
            pltpu.VMEM((tt + 8, GDN_QKV_W), F32),
            pltpu.VMEM((tt + POOL_HALO, POOL_W), F32),
            pltpu.VMEM((GDN_HEADS, GDN_DK, GDN_DV), F32),
        ],
        compiler_params=_params(("parallel", "arbitrary")),
        name="gdn_pool",
    )(qkv, z, p, ba, *small)


def _out_res_kernel(a_ref, w_ref, x_ref, mod_ref, o_ref, *, gate_row):
    gate = mod_ref[...][gate_row:gate_row + 1]
    o_ref[...] = x_ref[...] + gate * _mm(a_ref[...], w_ref[...])


def _out_res(a, w, x, mod, gate_row):
    b, t, d = x.shape
    k = a.shape[-1]
    tm = ROW_TILE
    return pl.pallas_call(
        functools.partial(_out_res_kernel, gate_row=gate_row),
        grid=(b, t // tm),
        in_specs=[
            pl.BlockSpec((None, tm, k), lambda bi, i: (bi, i, 0)),
            _resident(w.shape),
            pl.BlockSpec((None, tm, d), lambda bi, i: (bi, i, 0)),
            pl.BlockSpec((None, 6, d), lambda bi, i: (bi, 0, 0)),
        ],
        out_specs=pl.BlockSpec((None, tm, d), lambda bi, i: (bi, i, 0)),
        out_shape=jax.ShapeDtypeStruct((b, t, d), F32),
        compiler_params=_params(("parallel", "parallel")),
        name="out_proj_residual",
    )(a, w, x, mod)


def _odd_in_kernel(x_ref, mod_ref, gain_ref, w_ref, qn_ref, kn_ref, o0_ref, o1_ref, o2_ref,
                   hn_slab, h_scr):
    tm = x_ref.shape[0]
    d = x_ref.shape[1]
    m = mod_ref[...]
    hn = _mod_norm(x_ref[...], gain_ref[...], m[0:1], m[1:2])
    for gi, (_, dil) in enumerate(DIL_PATTERNS):
        if dil == 1:
            h_scr[gi] = hn.astype(BF16)
    for sb in range(d // LANES):
        hn_slab[sb] = hn[:, sb * LANES:(sb + 1) * LANES]
    for gi, (_, dil) in enumerate(DIL_PATTERNS):
        per = tm // dil
        for r in range(dil if dil > 1 else 0):
            for sb in range(d // LANES):
                h_scr[gi, r * per:(r + 1) * per, sb * LANES:(sb + 1) * LANES] = (
                    hn_slab[sb, pl.ds(r, per, stride=dil), :].astype(BF16))
    qn = qn_ref[...] * (ATT_DH ** -0.5 * LOG2_E)
    kn = kn_ref[...]
    half = ATT_W // 2
    for gi, o_ref in enumerate((o0_ref, o1_ref, o2_ref)):
        dil = DIL_PATTERNS[gi][1]
        per = tm // dil
        for blk in range(3 * ATT_W // half):
            lo = blk * half
            y = _mm(h_scr[gi], w_ref[:, gi * 3 * ATT_W + lo:gi * 3 * ATT_W + lo + half])
            part = lo // ATT_W
            for h in range(half // ATT_DH):
                cs = slice(lo + h * ATT_DH, lo + (h + 1) * ATT_DH)
                yh = y[:, h * ATT_DH:(h + 1) * ATT_DH]
                if part < 2:
                    ms = jnp.mean(yh * yh, axis=-1, keepdims=True)
                    yh = yh * lax.rsqrt(ms + RMS_EPS) * (qn if part == 0 else kn)
                yh = yh.astype(o_ref.dtype)
                for r in range(dil):
                    o_ref[r, :, cs] = yh[r * per:(r + 1) * per]


def _odd_in(x, mod, gain, w, q_norm, k_norm):
    b, t, d = x.shape
    tm = ROW_TILE
    dils = [dl for _, dl in DIL_PATTERNS]
    return pl.pallas_call(
        _odd_in_kernel,
        grid=(b, t // tm),
        in_specs=[
            pl.BlockSpec((None, tm, d), lambda bi, i: (bi, i, 0)),
            pl.BlockSpec((None, 6, d), lambda bi, i: (bi, 0, 0)),
            pl.BlockSpec((1, d), lambda bi, i: (0, 0)),
            _resident(w.shape),
            pl.BlockSpec((1, ATT_DH), lambda bi, i: (0, 0)),
            pl.BlockSpec((1, ATT_DH), lambda bi, i: (0, 0)),
        ],
        out_specs=[pl.BlockSpec((None, dl, tm // dl, 3 * ATT_W), lambda bi, i: (bi, 0, i, 0))
                   for dl in dils],
        out_shape=[jax.ShapeDtypeStruct((b, dl, t // dl, 3 * ATT_W), BF16) for dl in dils],
        scratch_shapes=[pltpu.VMEM((d // LANES, tm, LANES), F32),
                        pltpu.VMEM((len(dils), tm, d), BF16)],
        compiler_params=_params(("parallel", "parallel")),
        name="odd_in_proj",
    )(x, mod, gain, w, q_norm, k_norm)


def _attn_kernel(q_ref, kp_ref, kc_ref, vp_ref, vc_ref, o_ref, stat_ref, bias_scr, *, dil, n_back):
    blk = ATT_BLOCK
    first_step = jnp.logical_and(jnp.logical_and(pl.program_id(0) == 0, pl.program_id(1) == 0),
                                 pl.program_id(2) == 0)

    @pl.when(first_step)
    def _():
        a = lax.broadcasted_iota(jnp.int32, (blk, 2 * blk), 0)
        j = lax.broadcasted_iota(jnp.int32, (blk, 2 * blk), 1)
        rel = blk + a - j
        in_band = jnp.logical_and(rel >= 0, rel <= n_back)
        own = jnp.logical_and(in_band, j >= blk)
        relf = rel.astype(F32)
        for h in range(ATT_HEADS):
            bias = -(ATT_SLOPES[h] * dil * LOG2_E) * relf
            bias_scr[0, h] = jnp.where(own, bias, MASKED_SCORE)
            bias_scr[1, h] = jnp.where(in_band, bias, MASKED_SCORE)

    heads = [slice(h * ATT_DH, (h + 1) * ATT_DH) for h in range(ATT_HEADS)]
    lane = lax.broadcasted_iota(jnp.int32, (blk, LANES), 1)
    for i in range(q_ref.shape[0] // blk):
        rows = slice(i * blk, (i + 1) * blk)
        if i == 0:
            variant = jnp.minimum(pl.program_id(2), 1)
            keys = lambda hs: jnp.concatenate([kp_ref[:, hs], kc_ref[rows, hs]], axis=0)
            vals = lambda hs: jnp.concatenate([vp_ref[:, hs], vc_ref[rows, hs]], axis=0)
        else:
            variant = 1
            both = slice((i - 1) * blk, (i + 1) * blk)
            keys = lambda hs, both=both: kc_ref[both, hs]
            vals = lambda hs, both=both: vc_ref[both, hs]
        s = [_mm_nt(q_ref[rows, hs], keys(hs)) + bias_scr[variant, h] for h, hs in enumerate(heads)]
        m = [jnp.max(sh, axis=-1, keepdims=True) for sh in s]
        p = [jnp.exp2(sh - mh) for sh, mh in zip(s, m)]
        l = [jnp.sum(ph, axis=-1, keepdims=True) for ph in p]
        o = [_mm(ph.astype(BF16), vals(hs)) for ph, hs in zip(p, heads)]
        stats = jnp.zeros((blk, LANES), F32)
        for h, hs in enumerate(heads):
            o_ref[rows, hs] = o[h].astype(o_ref.dtype)
            stats = jnp.where(lane == h, m[h], stats)
            stats = jnp.where(lane == ATT_HEADS + h, l[h], stats)
        stat_ref[rows, :] = stats


def _attention_group(qkv, window, dil):
    b, _, length, _ = qkv.shape
    rows = ATT_QBLOCKS * ATT_BLOCK
    assert length % rows == 0

    def cur(part):
        return pl.BlockSpec((None, None, rows, ATT_W), lambda bi, r, n: (bi, r, n, part))

    def prev(part):
        return pl.BlockSpec((None, None, ATT_BLOCK, ATT_W),
                            lambda bi, r, n: (bi, r, jnp.maximum(n * ATT_QBLOCKS - 1, 0), part))

    return pl.pallas_call(
        functools.partial(_attn_kernel, dil=dil, n_back=window // dil),
        grid=(b, dil, length // rows),
        in_specs=[cur(0), prev(1), cur(1), prev(2), cur(2)],
        out_specs=[
            pl.BlockSpec((None, None, rows, ATT_W), lambda bi, r, n: (bi, r, n, 0)),
            pl.BlockSpec((None, None, rows, LANES), lambda bi, r, n: (bi, r, n, 0)),
        ],
        out_shape=[
            jax.ShapeDtypeStruct((b, dil, length, ATT_W), BF16),
            jax.ShapeDtypeStruct((b, dil, length, LANES), F32),
        ],
        scratch_shapes=[pltpu.VMEM((2, ATT_HEADS, ATT_BLOCK, 2 * ATT_BLOCK), F32)],
        compiler_params=_params(("arbitrary", "arbitrary", "arbitrary")),
        name=f"dilated_attention_d{dil}",
    )(qkv, qkv, qkv, qkv, qkv)


def _merge_out_kernel(o0_ref, o1_ref, o2_ref, l0_ref, l1_ref, l2_ref, w_ref, x_ref, mod_ref,
                      out_ref, o_slab, l_slab, a_scr):
    tm = x_ref.shape[0]
    o_refs = (o0_ref, o1_ref, o2_ref)
    l_refs = (l0_ref, l1_ref, l2_ref)
    heads = [slice(h * ATT_DH, (h + 1) * ATT_DH) for h in range(ATT_HEADS)]
    for gi, (_, dil) in enumerate(DIL_PATTERNS):
        if dil == 1:
            continue
        for r in range(dil):
            dst = pl.ds(r, tm // dil, stride=dil)
            l_slab[gi, dst, :] = l_refs[gi][r]
            for h, hs in enumerate(heads):
                o_slab[gi, h, dst, :] = o_refs[gi][r, :, hs].astype(F32)

    def lse_of(gi):
        return l_refs[gi][0] if DIL_PATTERNS[gi][1] == 1 else l_slab[gi]

    def out_of(gi, h):
        if DIL_PATTERNS[gi][1] == 1:
            return o_refs[gi][0, :, heads[h]].astype(F32)
        return o_slab[gi, h]

    stats = [lse_of(gi) for gi in range(len(DIL_PATTERNS))]
    dens = [pltpu.roll(st, LANES - ATT_HEADS, axis=1) for st in stats]
    mx = jnp.maximum(jnp.maximum(stats[0], stats[1]), stats[2])
    es = [jnp.exp2(st - mx) for st in stats]
    inv = 1.0 / (es[0] * dens[0] + es[1] * dens[1] + es[2] * dens[2])
    wts = [e * inv for e in es]
    for h, hs in enumerate(heads):
        merged = (_lane_bcast(wts[0], h) * out_of(0, h) + _lane_bcast(wts[1], h) * out_of(1, h)
                  + _lane_bcast(wts[2], h) * out_of(2, h))
        a_scr[:, hs] = merged.astype(BF16)
    gate = mod_ref[...][2:3]
    out_ref[...] = x_ref[...] + gate * _mm(a_scr[...], w_ref[...])


def _merge_out(outs, lses, w, x, mod):
    b, t, d = x.shape
    tm = ROW_TILE
    groups = len(DIL_PATTERNS)
    row = lambda width: pl.BlockSpec((None, tm, width), lambda bi, i: (bi, i, 0))
    by_residue = lambda dl, width: pl.BlockSpec((None, dl, tm // dl, width),
                                                lambda bi, i: (bi, 0, i, 0))
    dils = [dl for _, dl in DIL_PATTERNS]
    return pl.pallas_call(
        _merge_out_kernel,
        grid=(b, t // tm),
        in_specs=[by_residue(dl, ATT_W) for dl in dils] + [by_residue(dl, LANES) for dl in dils] + [
            _resident(w.shape), row(d), pl.BlockSpec((None, 6, d), lambda bi, i: (bi, 0, 0))],
        out_specs=row(d),
        out_shape=jax.ShapeDtypeStruct((b, t, d), F32),
        scratch_shapes=[
            pltpu.VMEM((groups, ATT_HEADS, tm, ATT_DH), F32),
            pltpu.VMEM((groups, tm, LANES), F32),
            pltpu.VMEM((tm, ATT_W), BF16),
        ],
        compiler_params=_params(("parallel", "parallel")),
        name="attn_merge_out_proj",
    )(*outs, *lses, w, x, mod)


def _ffn_up_kernel(x_ref, xh_ref, mod_ref, gain_ref, wg_ref, wv_ref, cw_ref, cb_ref, o_ref,
                   h_scr, g_scr, *, chunk):
    tm = x_ref.shape[0]
    halo = FFN_HALO
    m = mod_ref[...]
    gain = gain_ref[...]
    shift, scale = m[3:4], m[4:5]
    h_scr[halo:, :] = _mod_norm(x_ref[...], gain, shift, scale).astype(BF16)
    live = jnp.where(pl.program_id(1) > 0, 1.0, 0.0)
    h_scr[0:halo, :] = (_mod_norm(xh_ref[...], gain, shift, scale) * live).astype(BF16)
    cw = cw_ref[...]
    cb = cb_ref[...]
    width = o_ref.shape[-1]
    for lo in range(0, width, chunk):
        n = min(chunk, width - lo)
        cs = slice(lo, lo + n)
        g_scr[:, :n] = _mm(h_scr[...], wg_ref[:, cs])
        val = _mm(h_scr[halo:, :], wv_ref[:, cs])
        gate = cb[:, cs] + cw[FFN_CONV - 1:FFN_CONV, cs] * g_scr[halo:, :n]
        for j in range(FFN_CONV - 1):
            gate = gate + cw[j:j + 1, cs] * g_scr[pl.ds(halo - (FFN_CONV - 1) + j, tm), :n]
        o_ref[:, cs] = (_silu(gate) * val).astype(o_ref.dtype)


def _ffn_up(x, mod, gain, wg, wv, conv_w, conv_b):
    b, t, d = x.shape
    f = wg.shape[-1]
    tm = ROW_TILE
    chunk = 256
    per = tm // FFN_HALO
    return pl.pallas_call(
        functools.partial(_ffn_up_kernel, chunk=chunk),
        grid=(b, t // tm),
        in_specs=[
            pl.BlockSpec((None, tm, d), lambda bi, i: (bi, i, 0)),
            pl.BlockSpec((None, FFN_HALO, d), lambda bi, i: (bi, jnp.maximum(i * per - 1, 0), 0)),
            pl.BlockSpec((None, 6, d), lambda bi, i: (bi, 0, 0)),
            pl.BlockSpec((1, d), lambda bi, i: (0, 0)),
            _resident(wg.shape),
            _resident(wv.shape),
            pl.BlockSpec(conv_w.shape, lambda bi, i: (0, 0)),
            pl.BlockSpec(conv_b.shape, lambda bi, i: (0, 0)),
        ],
        out_specs=pl.BlockSpec((None, tm, f), lambda bi, i: (bi, i, 0)),
        out_shape=jax.ShapeDtypeStruct((b, t, f), BF16),
        scratch_shapes=[pltpu.VMEM((tm + FFN_HALO, d), BF16), pltpu.VMEM((tm + FFN_HALO, chunk), F32)],
        compiler_params=_params(("parallel", "parallel")),
        name="ffn_up_conv_gate",
    )(x, x, mod, gain, wg, wv, conv_w, conv_b)


def _lane_row(values, offset):
    return jnp.zeros((1, LANES), F32).at[0, offset:offset + values.shape[0]].set(values)


def kernel(x, c, ada_w, ada_b, norm_mix, norm_ffn, ev_w_in, ev_w_out, gdn_conv_w, gdn_a_log,
           gdn_dt_bias, gdn_norm, pool_w, pool_scale, od_w_in, od_w_out, att_q_norm, att_k_norm,
           ffn_w_up, ffn_conv_w, ffn_conv_b, ffn_w_down):
    b, t, d = x.shape
    depth = ada_w.shape[0]
    assert t % (ATT_BLOCK * max(dl for _, dl in DIL_PATTERNS)) == 0 and t % ROW_TILE == 0
    d_ff = ffn_w_down.shape[1]

    mod = _ada(c, ada_w, ada_b)

    i1 = GDN_QKV_W + GDN_V_W
    i2 = i1 + 2 * GDN_HEADS
    pad = jnp.zeros(ev_w_in.shape[:2] + (LANES - 2 * GDN_HEADS,), ev_w_in.dtype)
    ev_w = jnp.concatenate([ev_w_in[..., :i1], ev_w_in[..., i2:], ev_w_in[..., i1:i2], pad],
                           axis=-1).astype(BF16)

    for i in range(depth):
        m = mod[i]
        gain_m = norm_mix[i].reshape(1, d)
        if i % 2 == 0:
            e = i // 2
            qkv, z, p, ba = _even_in(x, m, gain_m, ev_w[e])
            oab = _gdn_pool(
                qkv, z, p, ba, gdn_conv_w[e],
                _lane_row(gdn_a_log[e], GDN_HEADS), _lane_row(gdn_dt_bias[e], GDN_HEADS),
                gdn_norm[e].reshape(1, GDN_DV), pool_w[e].astype(BF16),
                pool_scale[e].reshape(1, POOL_W))
            x = _out_res(oab, ev_w_out[e].astype(BF16), x, m, 2)
        else:
            o = i // 2
            qkvs = _odd_in(x, m, gain_m, od_w_in[o].astype(BF16),
                           att_q_norm[o].reshape(1, ATT_DH), att_k_norm[o].reshape(1, ATT_DH))
            outs, lses = [], []
            for qkv, (window, dil) in zip(qkvs, DIL_PATTERNS):
                og, lg = _attention_group(qkv, window, dil)
                outs.append(og)
                lses.append(lg)
            x = _merge_out(outs, lses, od_w_out[o].astype(BF16), x, m)

        w_up = ffn_w_up[i].astype(BF16)
        act = _ffn_up(x, m, norm_ffn[i].reshape(1, d), w_up[:, :d_ff], w_up[:, d_ff:],
                      ffn_conv_w[i], ffn_conv_b[i].reshape(1, d_ff))
        x = _out_res(act, ffn_w_down[i].astype(BF16), x, m, 5)
    return x
```

```python
import functools

import jax
import jax.numpy as jnp
from jax import lax
from jax.experimental import pallas as pl
from jax.experimental.pallas import tpu as pltpu

F32 = jnp.float32
BF16 = jnp.bfloat16

RMS_EPS = 1e-6

GDN_HEADS = 4
GDN_DK = 128
GDN_DV = 128
GDN_QK_W = GDN_HEADS * GDN_DK
GDN_V_W = GDN_HEADS * GDN_DV
GDN_QKV_W = 2 * GDN_QK_W + GDN_V_W
GDN_CONV = 4
GDN_CHUNK = 64
GDN_PAIR = 2 * GDN_CHUNK
GDN_TILE = 2 * GDN_PAIR
GDN_INV_BLOCK = 16

POOL_WINDOWS = (2, 4, 8, 16)
POOL_GROUP_W = 128
POOL_W = len(POOL_WINDOWS) * POOL_GROUP_W
POOL_HALO = 16

DIL_PATTERNS = ((128, 1), (512, 4), (2048, 16))
ATT_HEADS = 8
ATT_DH = 128
ATT_W = ATT_HEADS * ATT_DH
ATT_BLOCK = 128
ATT_QBLOCKS = 4
ATT_SLOPES = tuple(2.0 ** (-8.0 * (h + 1) / ATT_HEADS) for h in range(ATT_HEADS))
MASKED_SCORE = -1e30
LOG2_E = 1.4426950408889634
LN_2 = 0.6931471805599453

FFN_CONV = 3
FFN_HALO = 16
EVEN_HALO = 16
LANES = 128
ROW_TILE = 512
VMEM_LIMIT = 56 * 1024 * 1024


def _sigmoid(x):
    return 1.0 / (1.0 + jnp.exp(-x))


def _silu(x):
    return x * _sigmoid(x)


def _softplus(x):
    return jnp.maximum(x, 0.0) + jnp.log(1.0 + jnp.exp(-jnp.abs(x)))


def _mm(a, b):
    return jnp.dot(a, b, preferred_element_type=F32)


def _mm_nt(a, b):
    return lax.dot_general(a, b, (((1,), (1,)), ((), ())), preferred_element_type=F32)


def _mod_norm(x, gain, shift, scale):
    ms = jnp.mean(x * x, axis=-1, keepdims=True)
    return x * lax.rsqrt(ms + RMS_EPS) * (gain * (1.0 + scale)) + shift


def _params(semantics):
    return pltpu.CompilerParams(dimension_semantics=semantics, vmem_limit_bytes=VMEM_LIMIT)


def _resident(shape):
    nd = len(shape)
    return pl.BlockSpec(shape, lambda *_: (0,) * nd, pipeline_mode=pl.Buffered(1))


def _ada_kernel(c_ref, w_ref, b_ref, o_ref):
    cs = _silu(c_ref[...])
    o_ref[...] = _mm(cs.astype(BF16), w_ref[...].astype(BF16)) + b_ref[...]


def _ada(c, ada_w, ada_b):
    depth, d, n = ada_w.shape
    b = c.shape[0]
    rows = 8
    cp = jnp.pad(c, ((0, rows - b), (0, 0)))
    tn = 1536
    out = pl.pallas_call(
        _ada_kernel,
        grid=(depth, n // tn),
        in_specs=[
            pl.BlockSpec((rows, d), lambda l, j: (0, 0)),
            pl.BlockSpec((None, d, tn), lambda l, j: (l, 0, j)),
            pl.BlockSpec((None, 1, tn), lambda l, j: (l, 0, j)),
        ],
        out_specs=pl.BlockSpec((None, rows, tn), lambda l, j: (l, 0, j)),
        out_shape=jax.ShapeDtypeStruct((depth, rows, n), F32),
        compiler_params=_params(("parallel", "parallel")),
        name="ada_mod",
    )(cp, ada_w, ada_b.reshape(depth, 1, n))
    return out[:, :b].reshape(depth, b, 6, d)


def _even_in_kernel(x_ref, xh_ref, mod_ref, gain_ref, w_ref, cw_ref, qkv_ref, z_ref, p_ref, ba_ref,
                    h_scr, g_scr):
    tm = x_ref.shape[0]
    halo = EVEN_HALO
    m = mod_ref[...]
    gain = gain_ref[...]
    shift, scale = m[0:1], m[1:2]
    h_scr[halo:, :] = _mod_norm(x_ref[...], gain, shift, scale).astype(BF16)
    live = jnp.where(pl.program_id(1) > 0, 1.0, 0.0)
    h_scr[0:halo, :] = (_mod_norm(xh_ref[...], gain, shift, scale) * live).astype(BF16)

    cw = cw_ref[...]
    for part in range(3):
        cs = slice(part * GDN_QK_W, (part + 1) * GDN_QK_W)
        g_scr[...] = _mm(h_scr[...], w_ref[:, cs])
        acc = cw[GDN_CONV - 1:GDN_CONV, cs] * g_scr[halo:, :]
        for j in range(GDN_CONV - 1):
            acc = acc + cw[j:j + 1, cs] * g_scr[pl.ds(halo - (GDN_CONV - 1) + j, tm), :]
        act = _silu(acc)
        for h in range(GDN_HEADS):
            hs = slice(h * GDN_DK, (h + 1) * GDN_DK)
            a = act[:, hs]
            if part == 0:
                a = _l2norm(a) * (GDN_DK ** -0.5)
            elif part == 1:
                a = _l2norm(a)
            qkv_ref[:, part * GDN_QK_W + h * GDN_DK:part * GDN_QK_W + (h + 1) * GDN_DK] = (
                a.astype(qkv_ref.dtype))

    col = GDN_QKV_W
    for ref in (z_ref, p_ref, ba_ref):
        width = ref.shape[-1]
        ref[...] = _mm(h_scr[halo:, :], w_ref[:, col:col + width]).astype(ref.dtype)
        col += width


def _even_in(x, mod, gain, w, conv_w):
    b, t, d = x.shape
    tm = ROW_TILE
    per = tm // EVEN_HALO
    row = lambda width: pl.BlockSpec((None, tm, width), lambda bi, i: (bi, i, 0))
    return pl.pallas_call(
        _even_in_kernel,
        grid=(b, t // tm),
        in_specs=[
            row(d),
            pl.BlockSpec((None, EVEN_HALO, d), lambda bi, i: (bi, jnp.maximum(i * per - 1, 0), 0)),
            pl.BlockSpec((None, 6, d), lambda bi, i: (bi, 0, 0)),
            pl.BlockSpec((1, d), lambda bi, i: (0, 0)),
            _resident(w.shape),
            pl.BlockSpec(conv_w.shape, lambda bi, i: (0, 0)),
        ],
        out_specs=[row(GDN_QKV_W), row(GDN_V_W), row(POOL_W), row(LANES)],
        out_shape=[
            jax.ShapeDtypeStruct((b, t, GDN_QKV_W), BF16),
            jax.ShapeDtypeStruct((b, t, GDN_V_W), BF16),
            jax.ShapeDtypeStruct((b, t, POOL_W), BF16),
            jax.ShapeDtypeStruct((b, t, LANES), F32),
        ],
        scratch_shapes=[pltpu.VMEM((tm + EVEN_HALO, d), BF16),
                        pltpu.VMEM((tm + EVEN_HALO, GDN_QK_W), F32)],
        compiler_params=_params(("parallel", "parallel")),
        name="even_in_proj",
    )(x, x, mod, gain, w, conv_w)


def _lane_bcast(x, lane):
    return jnp.broadcast_to(x[:, lane:lane + 1], (x.shape[0], LANES))


def _row_bcast(x, row, rows):
    return jnp.broadcast_to(x[row:row + 1, :], (rows, x.shape[1]))


def _l2norm(x):
    return x * lax.rsqrt(jnp.sum(x * x, axis=-1, keepdims=True) + RMS_EPS)


def _unit_lower_solve(lows, rhss, blk, eye):
    bf = lambda xs: [x.astype(BF16) for x in xs]
    mm = lambda xs, ys: [_mm(x, y) for x, y in zip(xs, ys)]
    d = [jnp.where(blk, low, 0.0) for low in lows]
    n = bf([low - di for low, di in zip(lows, d)])
    db = bf(d)
    d2 = mm(db, db)
    d2b = bf(d2)
    d4 = mm(d2b, d2b)
    d4b = bf(d4)
    d8 = mm(d4b, d4b)
    acc = mm(bf([eye - x for x in d]), bf([eye + x for x in d2]))
    acc = mm(bf(acc), bf([eye + x for x in d4]))
    dinv = bf(mm(bf(acc), bf([eye + x for x in d8])))
    m = mm(dinv, n)
    mb = bf(m)
    m2 = mm(mb, mb)
    t1 = bf(mm(dinv, rhss))
    t2 = bf(mm(bf([eye + x for x in m2]), t1))
    return mm(bf([eye - x for x in m]), t2)


def _gdn_pool_kernel(qkv_ref, z_ref, p_ref, ba_ref, alog_ref, dtb_ref, gnorm_ref,
                     poolw_ref, pscale_ref, wout_ref, x_ref, mod_ref, o_ref, p_ext, s_scr, mix_scr):
    tt = GDN_TILE
    ch = GDN_CHUNK
    t = pl.program_id(1)

    @pl.when(t == 0)
    def _():
        p_ext[0:POOL_HALO, :] = jnp.zeros((POOL_HALO, POOL_W), F32)
        s_scr[...] = jnp.zeros(s_scr.shape, F32)

    p_ext[POOL_HALO:POOL_HALO + tt, :] = p_ref[...].astype(F32)

    pt = GDN_PAIR
    row = lax.broadcasted_iota(jnp.int32, (pt, pt), 0)
    col = lax.broadcasted_iota(jnp.int32, (pt, pt), 1)
    chunk_of = lambda idx: jnp.right_shift(idx, ch.bit_length() - 1)
    block_of = lambda idx: jnp.right_shift(idx, GDN_INV_BLOCK.bit_length() - 1)
    same = chunk_of(row) == chunk_of(col)
    causal = jnp.logical_and(same, row >= col)
    strict = jnp.logical_and(same, row > col)
    blk = block_of(row) == block_of(col)
    eye = jnp.where(row == col, 1.0, 0.0).astype(F32)
    first = row < ch
    tri = jnp.where(causal, 1.0, 0.0).astype(F32)

    pairs = tt // pt
    units = [(p, h) for p in range(pairs) for h in range(GDN_HEADS)]
    rows = lambda p: slice(p * pt, (p + 1) * pt)
    per_unit = lambda fn: [fn(p, h) for p, h in units]
    bf = lambda xs: [x.astype(BF16) for x in xs]

    beta_all, gc_all, gc_all_t = [], [], []
    for p in range(pairs):
        ba = ba_ref[rows(p), :]
        beta_all.append(_sigmoid(ba))
        g = -jnp.exp(alog_ref[...]) * _softplus(ba + dtb_ref[...])
        gc = jnp.dot(tri, g, preferred_element_type=F32, precision=lax.Precision.HIGHEST)
        gc_all.append(gc)
        gc_all_t.append(gc.T)

    q16 = per_unit(lambda p, h: qkv_ref[rows(p), h * GDN_DK:(h + 1) * GDN_DK])
    k16 = per_unit(lambda p, h: qkv_ref[rows(p), GDN_QK_W + h * GDN_DK:GDN_QK_W + (h + 1) * GDN_DK])
    v16 = per_unit(lambda p, h: qkv_ref[rows(p),
                                        2 * GDN_QK_W + h * GDN_DV:2 * GDN_QK_W + (h + 1) * GDN_DV])
    q = [x.astype(F32) for x in q16]
    k = [x.astype(F32) for x in k16]
    v = [x.astype(F32) for x in v16]
    beta = per_unit(lambda p, h: _lane_bcast(beta_all[p], h))
    gcol = per_unit(lambda p, h: _lane_bcast(gc_all[p], GDN_HEADS + h))
    grow = per_unit(lambda p, h: _row_bcast(gc_all_t[p], GDN_HEADS + h, pt))
    decay = [jnp.where(causal, jnp.exp(jnp.where(causal, gi - gj, 0.0)), 0.0)
             for gi, gj in zip(gcol, grow)]
    egc = [jnp.exp(g) for g in gcol]
    kb = [ki * bi for ki, bi in zip(k, beta)]
    low = [jnp.where(strict, _mm_nt(a, b) * d, 0.0) for a, b, d in zip(bf(kb), k16, decay)]
    intra = bf([_mm_nt(a, b) * d for a, b, d in zip(q16, k16, decay)])
    rhs = bf([jnp.concatenate([vi * bi, kbi * ei], axis=1)
              for vi, bi, kbi, ei in zip(v, beta, kb, egc)])
    uw = bf(_unit_lower_solve(low, rhs, blk, eye))

    iu = [_mm(a, b) for a, b in zip(intra, uw)]
    o_const = [x[:, :GDN_DV] for x in iu]
    q_eff = bf([qi * ei - x[:, GDN_DV:] for qi, ei, x in zip(q, egc, iu)])
    glast = [jnp.where(first, _row_bcast(g, ch - 1, pt), _row_bcast(g, pt - 1, pt)) for g in gcol]
    kdec = [ki * jnp.exp(gl - g) for ki, gl, g in zip(k, glast, gcol)]
    chunks = [slice(c * ch, (c + 1) * ch) for c in range(pt // ch)]
    ku = [[_mm(kd[rs].T.astype(BF16), x[rs]) for rs in chunks] for kd, x in zip(kdec, uw)]

    s = [s_scr[h] for h in range(GDN_HEADS)]
    outs = {}
    for p in range(pairs):
        for c, rs in enumerate(chunks):
            for h in range(GDN_HEADS):
                i = p * GDN_HEADS + h
                sb = s[h].astype(BF16)
                outs[p, h, c] = _mm(q_eff[i][rs], sb) + o_const[i][rs]
                gl = _row_bcast(egc[i], rs.stop - 1, GDN_DK)
                s[h] = gl * s[h] + ku[i][c][:, :GDN_DV] - _mm(ku[i][c][:, GDN_DV:].astype(BF16), sb)
    for h in range(GDN_HEADS):
        s_scr[h] = s[h]

    gnorm = gnorm_ref[...]
    for p, h in units:
        hs = slice(h * GDN_DV, (h + 1) * GDN_DV)
        o = jnp.concatenate([outs[p, h, c] for c in range(len(chunks))], axis=0)
        o = o * lax.rsqrt(jnp.mean(o * o, axis=-1, keepdims=True) + RMS_EPS) * gnorm
        mix_scr[rows(p), hs] = (o * _silu(z_ref[rows(p), hs].astype(F32))).astype(mix_scr.dtype)

    tok = (t * tt + 1 + lax.broadcasted_iota(jnp.int32, (tt, POOL_GROUP_W), 0)).astype(F32)
    pscale = pscale_ref[...]
    for gi, win in enumerate(POOL_WINDOWS):
        cs = slice(gi * POOL_GROUP_W, (gi + 1) * POOL_GROUP_W)
        cur = p_ext[POOL_HALO:POOL_HALO + tt, cs]
        tot = cur
        for back in range(1, win):
            tot = tot + p_ext[pl.ds(POOL_HALO - back, tt), cs]
        pooled = tot / jnp.minimum(tok, float(win)) - cur
        y = _mm(pooled.astype(BF16), poolw_ref[gi])
        mix_scr[:, GDN_V_W + gi * POOL_GROUP_W:GDN_V_W + (gi + 1) * POOL_GROUP_W] = (
            y * pscale[:, cs]).astype(mix_scr.dtype)

    p_ext[0:POOL_HALO, :] = p_ext[tt:tt + POOL_HALO, :]

    gate = mod_ref[...][2:3]
    o_ref[...] = x_ref[...] + gate * _mm(mix_scr[...], wout_ref[...])


def _gdn_pool(qkv, z, p, ba, alog_row, dtb_row, gnorm, pool_w, pool_scale, w_out, x, mod):
    b, t, d = x.shape
    tt = GDN_TILE
    row = lambda width: pl.BlockSpec((None, tt, width), lambda bi, i: (bi, i, 0))
    full = lambda a: pl.BlockSpec(a.shape, lambda bi, i: (0,) * a.ndim)
    small = (alog_row, dtb_row, gnorm, pool_w, pool_scale)
    return pl.pallas_call(
        _gdn_pool_kernel,
        grid=(b, t // tt),
        in_specs=[row(GDN_QKV_W), row(GDN_V_W), row(POOL_W), row(LANES)] + [full(a) for a in small] + [
            _resident(w_out.shape), row(d), pl.BlockSpec((None, 6, d), lambda bi, i: (bi, 0, 0))],
        out_specs=row(d),
        out_shape=jax.ShapeDtypeStruct((b, t, d), F32),
        scratch_shapes=[
            pltpu.VMEM((tt + POOL_HALO, POOL_W), F32),
            pltpu.VMEM((GDN_HEADS, GDN_DK, GDN_DV), F32),
            pltpu.VMEM((tt, GDN_V_W + POOL_W), BF16),
        ],
        compiler_params=_params(("parallel", "arbitrary")),
        name="gdn_pool_out_proj",
    )(qkv, z, p, ba, *small, w_out, x, mod)


def _out_res_kernel(a_ref, w_ref, x_ref, mod_ref, o_ref, *, gate_row):
    gate = mod_ref[...][gate_row:gate_row + 1]
    o_ref[...] = x_ref[...] + gate * _mm(a_ref[...], w_ref[...])


def _out_res(a, w, x, mod, gate_row):
    b, t, d = x.shape
    k = a.shape[-1]
    tm = ROW_TILE
    return pl.pallas_call(
        functools.partial(_out_res_kernel, gate_row=gate_row),
        grid=(b, t // tm),
        in_specs=[
            pl.BlockSpec((None, tm, k), lambda bi, i: (bi, i, 0)),
            _resident(w.shape),
            pl.BlockSpec((None, tm, d), lambda bi, i: (bi, i, 0)),
            pl.BlockSpec((None, 6, d), lambda bi, i: (bi, 0, 0)),
        ],
        out_specs=pl.BlockSpec((None, tm, d), lambda bi, i: (bi, i, 0)),
        out_shape=jax.ShapeDtypeStruct((b, t, d), F32),
        compiler_params=_params(("parallel", "parallel")),
        name="out_proj_residual",
    )(a, w, x, mod)


def _odd_in_kernel(x_ref, mod_ref, gain_ref, w_ref, qn_ref, kn_ref, o0_ref, o1_ref, o2_ref,
                   hn_slab, h_scr):
    tm = x_ref.shape[0]
    d = x_ref.shape[1]
    m = mod_ref[...]
    hn = _mod_norm(x_ref[...], gain_ref[...], m[0:1], m[1:2])
    for gi, (_, dil) in enumerate(DIL_PATTERNS):
        if dil == 1:
            h_scr[gi] = hn.astype(BF16)
    for sb in range(d // LANES):
        hn_slab[sb] = hn[:, sb * LANES:(sb + 1) * LANES]
    for gi, (_, dil) in enumerate(DIL_PATTERNS):
        per = tm // dil
        for r in range(dil if dil > 1 else 0):
            for sb in range(d // LANES):
                h_scr[gi, r * per:(r + 1) * per, sb * LANES:(sb + 1) * LANES] = (
                    hn_slab[sb, pl.ds(r, per, stride=dil), :].astype(BF16))
    qn = qn_ref[...] * (ATT_DH ** -0.5 * LOG2_E)
    kn = kn_ref[...]
    half = ATT_W // 2
    for gi, o_ref in enumerate((o0_ref, o1_ref, o2_ref)):
        dil = DIL_PATTERNS[gi][1]
        per = tm // dil
        for blk in range(3 * ATT_W // half):
            lo = blk * half
            y = _mm(h_scr[gi], w_ref[:, gi * 3 * ATT_W + lo:gi * 3 * ATT_W + lo + half])
            part = lo // ATT_W
            for h in range(half // ATT_DH):
                cs = slice(lo + h * ATT_DH, lo + (h + 1) * ATT_DH)
                yh = y[:, h * ATT_DH:(h + 1) * ATT_DH]
                if part < 2:
                    ms = jnp.mean(yh * yh, axis=-1, keepdims=True)
                    yh = yh * lax.rsqrt(ms + RMS_EPS) * (qn if part == 0 else kn)
                yh = yh.astype(o_ref.dtype)
                for r in range(dil):
                    o_ref[r, :, cs] = yh[r * per:(r + 1) * per]


def _odd_in(x, mod, gain, w, q_norm, k_norm):
    b, t, d = x.shape
    tm = ROW_TILE
    dils = [dl for _, dl in DIL_PATTERNS]
    return pl.pallas_call(
        _odd_in_kernel,
        grid=(b, t // tm),
        in_specs=[
            pl.BlockSpec((None, tm, d), lambda bi, i: (bi, i, 0)),
            pl.BlockSpec((None, 6, d), lambda bi, i: (bi, 0, 0)),
            pl.BlockSpec((1, d), lambda bi, i: (0, 0)),
            _resident(w.shape),
            pl.BlockSpec((1, ATT_DH), lambda bi, i: (0, 0)),
            pl.BlockSpec((1, ATT_DH), lambda bi, i: (0, 0)),
        ],
        out_specs=[pl.BlockSpec((None, dl, tm // dl, 3 * ATT_W), lambda bi, i: (bi, 0, i, 0))
                   for dl in dils],
        out_shape=[jax.ShapeDtypeStruct((b, dl, t // dl, 3 * ATT_W), BF16) for dl in dils],
        scratch_shapes=[pltpu.VMEM((d // LANES, tm, LANES), F32),
                        pltpu.VMEM((len(dils), tm, d), BF16)],
        compiler_params=_params(("parallel", "parallel")),
        name="odd_in_proj",
    )(x, mod, gain, w, q_norm, k_norm)


def _attn_kernel(q_ref, kp_ref, kc_ref, vp_ref, vc_ref, o_ref, stat_ref, bias_scr, *, dil, n_back):
    blk = ATT_BLOCK
    first_step = jnp.logical_and(jnp.logical_and(pl.program_id(0) == 0, pl.program_id(1) == 0),
                                 pl.program_id(2) == 0)

    @pl.when(first_step)
    def _():
        a = lax.broadcasted_iota(jnp.int32, (blk, 2 * blk), 0)
        j = lax.broadcasted_iota(jnp.int32, (blk, 2 * blk), 1)
        rel = blk + a - j
        in_band = jnp.logical_and(rel >= 0, rel <= n_back)
        own = jnp.logical_and(in_band, j >= blk)
        relf = rel.astype(F32)
        for h in range(ATT_HEADS):
            bias = -(ATT_SLOPES[h] * dil * LOG2_E) * relf
            bias_scr[0, h] = jnp.where(own, bias, MASKED_SCORE)
            bias_scr[1, h] = jnp.where(in_band, bias, MASKED_SCORE)

    heads = [slice(h * ATT_DH, (h + 1) * ATT_DH) for h in range(ATT_HEADS)]
    lane = lax.broadcasted_iota(jnp.int32, (blk, LANES), 1)
    for i in range(q_ref.shape[0] // blk):
        rows = slice(i * blk, (i + 1) * blk)
        if i == 0:
            variant = jnp.minimum(pl.program_id(2), 1)
            keys = lambda hs: jnp.concatenate([kp_ref[:, hs], kc_ref[rows, hs]], axis=0)
            vals = lambda hs: jnp.concatenate([vp_ref[:, hs], vc_ref[rows, hs]], axis=0)
        else:
            variant = 1
            both = slice((i - 1) * blk, (i + 1) * blk)
            keys = lambda hs, both=both: kc_ref[both, hs]
            vals = lambda hs, both=both: vc_ref[both, hs]
        s = [_mm_nt(q_ref[rows, hs], keys(hs)) + bias_scr[variant, h] for h, hs in enumerate(heads)]
        m = [jnp.max(sh, axis=-1, keepdims=True) for sh in s]
        p = [jnp.exp2(sh - mh) for sh, mh in zip(s, m)]
        l = [jnp.sum(ph, axis=-1, keepdims=True) for ph in p]
        o = [_mm(ph.astype(BF16), vals(hs)) for ph, hs in zip(p, heads)]
        stats = jnp.zeros((blk, LANES), F32)
        for h, hs in enumerate(heads):
            o_ref[rows, hs] = o[h].astype(o_ref.dtype)
            stats = jnp.where(lane == h, m[h], stats)
            stats = jnp.where(lane == ATT_HEADS + h, l[h], stats)
        stat_ref[rows, :] = stats


def _attention_group(qkv, window, dil):
    b, _, length, _ = qkv.shape
    rows = ATT_QBLOCKS * ATT_BLOCK
    assert length % rows == 0

    def cur(part):
        return pl.BlockSpec((None, None, rows, ATT_W), lambda bi, r, n: (bi, r, n, part))

    def prev(part):
        return pl.BlockSpec((None, None, ATT_BLOCK, ATT_W),
                            lambda bi, r, n: (bi, r, jnp.maximum(n * ATT_QBLOCKS - 1, 0), part))

    return pl.pallas_call(
        functools.partial(_attn_kernel, dil=dil, n_back=window // dil),
        grid=(b, dil, length // rows),
        in_specs=[cur(0), prev(1), cur(1), prev(2), cur(2)],
        out_specs=[
            pl.BlockSpec((None, None, rows, ATT_W), lambda bi, r, n: (bi, r, n, 0)),
            pl.BlockSpec((None, None, rows, LANES), lambda bi, r, n: (bi, r, n, 0)),
        ],
        out_shape=[
            jax.ShapeDtypeStruct((b, dil, length, ATT_W), BF16),
            jax.ShapeDtypeStruct((b, dil, length, LANES), F32),
        ],
        scratch_shapes=[pltpu.VMEM((2, ATT_HEADS, ATT_BLOCK, 2 * ATT_BLOCK), F32)],
        compiler_params=_params(("arbitrary", "arbitrary", "arbitrary")),
        name=f"dilated_attention_d{dil}",
    )(qkv, qkv, qkv, qkv, qkv)


def _merge_out_kernel(o0_ref, o1_ref, o2_ref, l0_ref, l1_ref, l2_ref, w_ref, x_ref, mod_ref,
                      out_ref, o_slab, l_slab, a_scr):
    tm = x_ref.shape[0]
    o_refs = (o0_ref, o1_ref, o2_ref)
    l_refs = (l0_ref, l1_ref, l2_ref)
    heads = [slice(h * ATT_DH, (h + 1) * ATT_DH) for h in range(ATT_HEADS)]
    for gi, (_, dil) in enumerate(DIL_PATTERNS):
        if dil == 1:
            continue
        for r in range(dil):
            dst = pl.ds(r, tm // dil, stride=dil)
            l_slab[gi, dst, :] = l_refs[gi][r]
            for h, hs in enumerate(heads):
                o_slab[gi, h, dst, :] = o_refs[gi][r, :, hs].astype(F32)

    def lse_of(gi):
        return l_refs[gi][0] if DIL_PATTERNS[gi][1] == 1 else l_slab[gi]

    def out_of(gi, h):
        if DIL_PATTERNS[gi][1] == 1:
            return o_refs[gi][0, :, heads[h]].astype(F32)
        return o_slab[gi, h]

    stats = [lse_of(gi) for gi in range(len(DIL_PATTERNS))]
    dens = [pltpu.roll(st, LANES - ATT_HEADS, axis=1) for st in stats]
    mx = jnp.maximum(jnp.maximum(stats[0], stats[1]), stats[2])
    es = [jnp.exp2(st - mx) for st in stats]
    inv = 1.0 / (es[0] * dens[0] + es[1] * dens[1] + es[2] * dens[2])
    wts = [e * inv for e in es]
    for h, hs in enumerate(heads):
        merged = (_lane_bcast(wts[0], h) * out_of(0, h) + _lane_bcast(wts[1], h) * out_of(1, h)
                  + _lane_bcast(wts[2], h) * out_of(2, h))
        a_scr[:, hs] = merged.astype(BF16)
    gate = mod_ref[...][2:3]
    out_ref[...] = x_ref[...] + gate * _mm(a_scr[...], w_ref[...])


def _merge_out(outs, lses, w, x, mod):
    b, t, d = x.shape
    tm = ROW_TILE
    groups = len(DIL_PATTERNS)
    row = lambda width: pl.BlockSpec((None, tm, width), lambda bi, i: (bi, i, 0))
    by_residue = lambda dl, width: pl.BlockSpec((None, dl, tm // dl, width),
                                                lambda bi, i: (bi, 0, i, 0))
    dils = [dl for _, dl in DIL_PATTERNS]
    return pl.pallas_call(
        _merge_out_kernel,
        grid=(b, t // tm),
        in_specs=[by_residue(dl, ATT_W) for dl in dils] + [by_residue(dl, LANES) for dl in dils] + [
            _resident(w.shape), row(d), pl.BlockSpec((None, 6, d), lambda bi, i: (bi, 0, 0))],
        out_specs=row(d),
        out_shape=jax.ShapeDtypeStruct((b, t, d), F32),
        scratch_shapes=[
            pltpu.VMEM((groups, ATT_HEADS, tm, ATT_DH), F32),
            pltpu.VMEM((groups, tm, LANES), F32),
            pltpu.VMEM((tm, ATT_W), BF16),
        ],
        compiler_params=_params(("parallel", "parallel")),
        name="attn_merge_out_proj",
    )(*outs, *lses, w, x, mod)


def _ffn_up_kernel(x_ref, xh_ref, mod_ref, gain_ref, wg_ref, wv_ref, cw_ref, cb_ref, o_ref,
                   h_scr, g_scr, *, chunk):
    tm = x_ref.shape[0]
    halo = FFN_HALO
    m = mod_ref[...]
    gain = gain_ref[...]
    shift, scale = m[3:4], m[4:5]
    h_scr[halo:, :] = _mod_norm(x_ref[...], gain, shift, scale).astype(BF16)
    live = jnp.where(pl.program_id(1) > 0, 1.0, 0.0)
    h_scr[0:halo, :] = (_mod_norm(xh_ref[...], gain, shift, scale) * live).astype(BF16)
    cw = cw_ref[...]
    cb = cb_ref[...]
    for c in range(o_ref.shape[-1] // chunk):
        cs = slice(c * chunk, (c + 1) * chunk)
        g_scr[...] = _mm(h_scr[...], wg_ref[:, cs])
        val = _mm(h_scr[halo:, :], wv_ref[:, cs])
        gate = cb[:, cs] + cw[FFN_CONV - 1:FFN_CONV, cs] * g_scr[halo:, :]
        for j in range(FFN_CONV - 1):
            gate = gate + cw[j:j + 1, cs] * g_scr[pl.ds(halo - (FFN_CONV - 1) + j, tm), :]
        o_ref[:, cs] = (_silu(gate) * val).astype(o_ref.dtype)


def _ffn_up(x, mod, gain, w_up, conv_w, conv_b):
    b, t, d = x.shape
    f = w_up.shape[-1] // 2
    tm = ROW_TILE
    chunk = 256
    per = tm // FFN_HALO
    half = lambda j: pl.BlockSpec((d, f), lambda bi, i: (0, j), pipeline_mode=pl.Buffered(1))
    return pl.pallas_call(
        functools.partial(_ffn_up_kernel, chunk=chunk),
        grid=(b, t // tm),
        in_specs=[
            pl.BlockSpec((None, tm, d), lambda bi, i: (bi, i, 0)),
            pl.BlockSpec((None, FFN_HALO, d), lambda bi, i: (bi, jnp.maximum(i * per - 1, 0), 0)),
            pl.BlockSpec((None, 6, d), lambda bi, i: (bi, 0, 0)),
            pl.BlockSpec((1, d), lambda bi, i: (0, 0)),
            half(0),
            half(1),
            pl.BlockSpec(conv_w.shape, lambda bi, i: (0, 0)),
            pl.BlockSpec(conv_b.shape, lambda bi, i: (0, 0)),
        ],
        out_specs=pl.BlockSpec((None, tm, f), lambda bi, i: (bi, i, 0)),
        out_shape=jax.ShapeDtypeStruct((b, t, f), BF16),
        scratch_shapes=[pltpu.VMEM((tm + FFN_HALO, d), BF16), pltpu.VMEM((tm + FFN_HALO, chunk), F32)],
        compiler_params=_params(("parallel", "parallel")),
        name="ffn_up_conv_gate",
    )(x, x, mod, gain, w_up, w_up, conv_w, conv_b)


def _lane_row(values, offset):
    return jnp.zeros((1, LANES), F32).at[0, offset:offset + values.shape[0]].set(values)


def kernel(x, c, ada_w, ada_b, norm_mix, norm_ffn, ev_w_in, ev_w_out, gdn_conv_w, gdn_a_log,
           gdn_dt_bias, gdn_norm, pool_w, pool_scale, od_w_in, od_w_out, att_q_norm, att_k_norm,
           ffn_w_up, ffn_conv_w, ffn_conv_b, ffn_w_down):
    b, t, d = x.shape
    depth = ada_w.shape[0]
    assert t % (ATT_BLOCK * max(dl for _, dl in DIL_PATTERNS)) == 0 and t % ROW_TILE == 0
    d_ff = ffn_w_down.shape[1]

    mod = _ada(c, ada_w, ada_b)

    i1 = GDN_QKV_W + GDN_V_W
    i2 = i1 + 2 * GDN_HEADS
    pad = jnp.zeros(ev_w_in.shape[:2] + (LANES - 2 * GDN_HEADS,), ev_w_in.dtype)
    ev_w = jnp.concatenate([ev_w_in[..., :i1], ev_w_in[..., i2:], ev_w_in[..., i1:i2], pad],
                           axis=-1).astype(BF16)

    for i in range(depth):
        m = mod[i]
        gain_m = norm_mix[i].reshape(1, d)
        if i % 2 == 0:
            e = i // 2
            qkv, z, p, ba = _even_in(x, m, gain_m, ev_w[e], gdn_conv_w[e])
            x = _gdn_pool(
                qkv, z, p, ba,
                _lane_row(gdn_a_log[e], GDN_HEADS), _lane_row(gdn_dt_bias[e], GDN_HEADS),
                gdn_norm[e].reshape(1, GDN_DV), pool_w[e].astype(BF16),
                pool_scale[e].reshape(1, POOL_W), ev_w_out[e].astype(BF16), x, m)
        else:
            o = i // 2
            qkvs = _odd_in(x, m, gain_m, od_w_in[o].astype(BF16),
                           att_q_norm[o].reshape(1, ATT_DH), att_k_norm[o].reshape(1, ATT_DH))
            outs, lses = [], []
            for qkv, (window, dil) in zip(qkvs, DIL_PATTERNS):
                og, lg = _attention_group(qkv, window, dil)
                outs.append(og)
                lses.append(lg)
            x = _merge_out(outs, lses, od_w_out[o].astype(BF16), x, m)

        act = _ffn_up(x, m, norm_ffn[i].reshape(1, d), ffn_w_up[i].astype(BF16),
                      ffn_conv_w[i], ffn_conv_b[i].reshape(1, d_ff))
        x = _out_res(act, ffn_w_down[i].astype(BF16), x, m, 5)
    return x
```

```python
import functools

import jax
import jax.numpy as jnp
from jax import lax
from jax.experimental import pallas as pl
from jax.experimental.pallas import tpu as pltpu

F32 = jnp.float32
BF16 = jnp.bfloat16

RMS_EPS = 1e-6

GDN_HEADS = 4
GDN_DK = 128
GDN_DV = 128
GDN_QK_W = GDN_HEADS * GDN_DK
GDN_V_W = GDN_HEADS * GDN_DV
GDN_QKV_W = 2 * GDN_QK_W + GDN_V_W
GDN_CONV = 4
GDN_CHUNK = 64
GDN_PAIR = 2 * GDN_CHUNK
GDN_TILE = 2 * GDN_PAIR
GDN_INV_BLOCK = 16

POOL_WINDOWS = (2, 4, 8, 16)
POOL_GROUP_W = 128
POOL_W = len(POOL_WINDOWS) * POOL_GROUP_W
POOL_HALO = 16

DIL_PATTERNS = ((128, 1), (512, 4), (2048, 16))
ATT_HEADS = 8
ATT_DH = 128
ATT_W = ATT_HEADS * ATT_DH
ATT_BLOCK = 128
ATT_QBLOCKS = 8
ATT_SLOPES = tuple(2.0 ** (-8.0 * (h + 1) / ATT_HEADS) for h in range(ATT_HEADS))
MASKED_SCORE = -1e30
LOG2_E = 1.4426950408889634
LN_2 = 0.6931471805599453

FFN_CONV = 3
FFN_HALO = 16
EVEN_HALO = 16
LANES = 128
ROW_TILE = 512
VMEM_LIMIT = 56 * 1024 * 1024


def _sigmoid(x):
    return 1.0 / (1.0 + jnp.exp(-x))


def _silu(x):
    return x * _sigmoid(x)


def _softplus(x):
    return jnp.maximum(x, 0.0) + jnp.log(1.0 + jnp.exp(-jnp.abs(x)))


def _mm(a, b):
    return jnp.dot(a, b, preferred_element_type=F32)


def _mm_nt(a, b):
    return lax.dot_general(a, b, (((1,), (1,)), ((), ())), preferred_element_type=F32)


def _mod_norm(x, gain, shift, scale):
    ms = jnp.mean(x * x, axis=-1, keepdims=True)
    return x * lax.rsqrt(ms + RMS_EPS) * (gain * (1.0 + scale)) + shift


def _params(semantics):
    return pltpu.CompilerParams(dimension_semantics=semantics, vmem_limit_bytes=VMEM_LIMIT)


def _resident(stack, layer, col_blocks=1, col=0):
    _, k, n = stack.shape
    return pl.BlockSpec((None, k, n // col_blocks), lambda *_: (layer, 0, col),
                        pipeline_mode=pl.Buffered(1))


def _ada_kernel(c_ref, w_ref, b_ref, o_ref):
    cs = _silu(c_ref[...])
    o_ref[...] = _mm(cs.astype(BF16), w_ref[...].astype(BF16)) + b_ref[...]


def _ada(c, ada_w, ada_b):
    depth, d, n = ada_w.shape
    b = c.shape[0]
    rows = 8
    cp = jnp.pad(c, ((0, rows - b), (0, 0)))
    tn = 1536
    out = pl.pallas_call(
        _ada_kernel,
        grid=(depth, n // tn),
        in_specs=[
            pl.BlockSpec((rows, d), lambda l, j: (0, 0)),
            pl.BlockSpec((None, d, tn), lambda l, j: (l, 0, j)),
            pl.BlockSpec((None, 1, tn), lambda l, j: (l, 0, j)),
        ],
        out_specs=pl.BlockSpec((None, rows, tn), lambda l, j: (l, 0, j)),
        out_shape=jax.ShapeDtypeStruct((depth, rows, n), F32),
        compiler_params=_params(("parallel", "parallel")),
        name="ada_mod",
    )(cp, ada_w, ada_b.reshape(depth, 1, n))
    return out[:, :b].reshape(depth, b, 6, d)


def _even_in_kernel(x_ref, xh_ref, mod_ref, gain_ref, w_ref, cw_ref, qkv_ref, z_ref, p_ref, ba_ref,
                    h_scr, g_scr):
    tm = x_ref.shape[0]
    halo = EVEN_HALO
    m = mod_ref[...]
    gain = gain_ref[...]
    shift, scale = m[0:1], m[1:2]
    h_scr[halo:, :] = _mod_norm(x_ref[...], gain, shift, scale).astype(BF16)
    live = jnp.where(pl.program_id(1) > 0, 1.0, 0.0)
    h_scr[0:halo, :] = (_mod_norm(xh_ref[...], gain, shift, scale) * live).astype(BF16)

    cw = cw_ref[...]
    width = g_scr.shape[-1]
    for lo in range(0, GDN_QKV_W, width):
        cs = slice(lo, lo + width)
        part = lo // GDN_QK_W
        g_scr[...] = _mm(h_scr[...], w_ref[:, cs])
        acc = cw[GDN_CONV - 1:GDN_CONV, cs] * g_scr[halo:, :]
        for j in range(GDN_CONV - 1):
            acc = acc + cw[j:j + 1, cs] * g_scr[pl.ds(halo - (GDN_CONV - 1) + j, tm), :]
        act = _silu(acc)
        for h in range(width // GDN_DK):
            a = act[:, h * GDN_DK:(h + 1) * GDN_DK]
            if part == 0:
                a = _l2norm(a) * (GDN_DK ** -0.5)
            elif part == 1:
                a = _l2norm(a)
            qkv_ref[:, lo + h * GDN_DK:lo + (h + 1) * GDN_DK] = a.astype(qkv_ref.dtype)

    col = GDN_QKV_W
    for ref in (z_ref, p_ref, ba_ref):
        n = ref.shape[-1]
        ref[...] = _mm(h_scr[halo:, :], w_ref[:, col:col + n]).astype(ref.dtype)
        col += n


def _even_in(x, mod, gain, w, layer, conv_w):
    b, t, d = x.shape
    tm = ROW_TILE
    per = tm // EVEN_HALO
    row = lambda width: pl.BlockSpec((None, tm, width), lambda bi, i: (bi, i, 0))
    return pl.pallas_call(
        _even_in_kernel,
        grid=(b, t // tm),
        in_specs=[
            row(d),
            pl.BlockSpec((None, EVEN_HALO, d), lambda bi, i: (bi, jnp.maximum(i * per - 1, 0), 0)),
            pl.BlockSpec((None, 6, d), lambda bi, i: (bi, 0, 0)),
            pl.BlockSpec((1, d), lambda bi, i: (0, 0)),
            _resident(w, layer),
            pl.BlockSpec(conv_w.shape, lambda bi, i: (0, 0)),
        ],
        out_specs=[row(GDN_QKV_W), row(GDN_V_W), row(POOL_W), row(LANES)],
        out_shape=[
            jax.ShapeDtypeStruct((b, t, GDN_QKV_W), BF16),
            jax.ShapeDtypeStruct((b, t, GDN_V_W), BF16),
            jax.ShapeDtypeStruct((b, t, POOL_W), BF16),
            jax.ShapeDtypeStruct((b, t, LANES), F32),
        ],
        scratch_shapes=[pltpu.VMEM((tm + EVEN_HALO, d), BF16),
                        pltpu.VMEM((tm + EVEN_HALO, 2 * GDN_DK), F32)],
        compiler_params=_params(("parallel", "parallel")),
        name="even_in_proj",
    )(x, x, mod, gain, w, conv_w)


def _lane_bcast(x, lane):
    return jnp.broadcast_to(x[:, lane:lane + 1], (x.shape[0], LANES))


def _row_bcast(x, row, rows):
    return jnp.broadcast_to(x[row:row + 1, :], (rows, x.shape[1]))


def _l2norm(x):
    return x * lax.rsqrt(jnp.sum(x * x, axis=-1, keepdims=True) + RMS_EPS)


def _unit_lower_solve(lows, rhss, blk, eye):
    bf = lambda xs: [x.astype(BF16) for x in xs]
    mm = lambda xs, ys: [_mm(x, y) for x, y in zip(xs, ys)]
    d = [jnp.where(blk, low, 0.0) for low in lows]
    n = bf([low - di for low, di in zip(lows, d)])
    db = bf(d)
    d2 = mm(db, db)
    d2b = bf(d2)
    d4 = mm(d2b, d2b)
    d4b = bf(d4)
    d8 = mm(d4b, d4b)
    acc = mm(bf([eye - x for x in d]), bf([eye + x for x in d2]))
    acc = mm(bf(acc), bf([eye + x for x in d4]))
    dinv = bf(mm(bf(acc), bf([eye + x for x in d8])))
    m = mm(dinv, n)
    mb = bf(m)
    m2 = mm(mb, mb)
    t1 = bf(mm(dinv, rhss))
    t2 = bf(mm(bf([eye + x for x in m2]), t1))
    return mm(bf([eye - x for x in m]), t2)


def _gdn_pool_kernel(qkv_ref, z_ref, p_ref, ba_ref, alog_ref, dtb_ref, gnorm_ref,
                     poolw_ref, pscale_ref, wout_ref, x_ref, mod_ref, o_ref, p_ext, s_scr, mix_scr):
    tt = GDN_TILE
    ch = GDN_CHUNK
    t = pl.program_id(1)

    @pl.when(t == 0)
    def _():
        p_ext[0:POOL_HALO, :] = jnp.zeros((POOL_HALO, POOL_W), F32)
        s_scr[...] = jnp.zeros(s_scr.shape, F32)

    p_ext[POOL_HALO:POOL_HALO + tt, :] = p_ref[...].astype(F32)

    pt = GDN_PAIR
    row = lax.broadcasted_iota(jnp.int32, (pt, pt), 0)
    col = lax.broadcasted_iota(jnp.int32, (pt, pt), 1)
    chunk_of = lambda idx: jnp.right_shift(idx, ch.bit_length() - 1)
    block_of = lambda idx: jnp.right_shift(idx, GDN_INV_BLOCK.bit_length() - 1)
    same = chunk_of(row) == chunk_of(col)
    causal = jnp.logical_and(same, row >= col)
    strict = jnp.logical_and(same, row > col)
    blk = block_of(row) == block_of(col)
    eye = jnp.where(row == col, 1.0, 0.0).astype(F32)
    first = row < ch
    tri = jnp.where(causal, 1.0, 0.0).astype(F32)

    pairs = tt // pt
    units = [(p, h) for p in range(pairs) for h in range(GDN_HEADS)]
    rows = lambda p: slice(p * pt, (p + 1) * pt)
    per_unit = lambda fn: [fn(p, h) for p, h in units]
    bf = lambda xs: [x.astype(BF16) for x in xs]

    beta_all, gc_all, gc_all_t = [], [], []
    for p in range(pairs):
        ba = ba_ref[rows(p), :]
        beta_all.append(_sigmoid(ba))
        g = -jnp.exp(alog_ref[...]) * _softplus(ba + dtb_ref[...])
        gc = jnp.dot(tri, g, preferred_element_type=F32, precision=lax.Precision.HIGHEST)
        gc_all.append(gc)
        gc_all_t.append(gc.T)

    q16 = per_unit(lambda p, h: qkv_ref[rows(p), h * GDN_DK:(h + 1) * GDN_DK])
    k16 = per_unit(lambda p, h: qkv_ref[rows(p), GDN_QK_W + h * GDN_DK:GDN_QK_W + (h + 1) * GDN_DK])
    v16 = per_unit(lambda p, h: qkv_ref[rows(p),
                                        2 * GDN_QK_W + h * GDN_DV:2 * GDN_QK_W + (h + 1) * GDN_DV])
    q = [x.astype(F32) for x in q16]
    k = [x.astype(F32) for x in k16]
    v = [x.astype(F32) for x in v16]
    beta = per_unit(lambda p, h: _lane_bcast(beta_all[p], h))
    gcol = per_unit(lambda p, h: _lane_bcast(gc_all[p], GDN_HEADS + h))
    grow = per_unit(lambda p, h: _row_bcast(gc_all_t[p], GDN_HEADS + h, pt))
    decay = [jnp.where(causal, jnp.exp(jnp.where(causal, gi - gj, 0.0)), 0.0)
             for gi, gj in zip(gcol, grow)]
    egc = [jnp.exp(g) for g in gcol]
    kb = [ki * bi for ki, bi in zip(k, beta)]
    low = [jnp.where(strict, _mm_nt(a, b) * d, 0.0) for a, b, d in zip(bf(kb), k16, decay)]
    intra = bf([_mm_nt(a, b) * d for a, b, d in zip(q16, k16, decay)])
    rhs = bf([jnp.concatenate([vi * bi, kbi * ei], axis=1)
              for vi, bi, kbi, ei in zip(v, beta, kb, egc)])
    uw = bf(_unit_lower_solve(low, rhs, blk, eye))

    iu = [_mm(a, b) for a, b in zip(intra, uw)]
    o_const = [x[:, :GDN_DV] for x in iu]
    q_eff = bf([qi * ei - x[:, GDN_DV:] for qi, ei, x in zip(q, egc, iu)])
    glast = [jnp.where(first, _row_bcast(g, ch - 1, pt), _row_bcast(g, pt - 1, pt)) for g in gcol]
    kdec = [ki * jnp.exp(gl - g) for ki, gl, g in zip(k, glast, gcol)]
    chunks = [slice(c * ch, (c + 1) * ch) for c in range(pt // ch)]
    ku = [[_mm(kd[rs].T.astype(BF16), x[rs]) for rs in chunks] for kd, x in zip(kdec, uw)]

    s = [s_scr[h] for h in range(GDN_HEADS)]
    outs = {}
    for p in range(pairs):
        for c, rs in enumerate(chunks):
            for h in range(GDN_HEADS):
                i = p * GDN_HEADS + h
                sb = s[h].astype(BF16)
                outs[p, h, c] = _mm(q_eff[i][rs], sb) + o_const[i][rs]
                gl = _row_bcast(egc[i], rs.stop - 1, GDN_DK)
                s[h] = gl * s[h] + ku[i][c][:, :GDN_DV] - _mm(ku[i][c][:, GDN_DV:].astype(BF16), sb)
    for h in range(GDN_HEADS):
        s_scr[h] = s[h]

    gnorm = gnorm_ref[...]
    for p, h in units:
        hs = slice(h * GDN_DV, (h + 1) * GDN_DV)
        o = jnp.concatenate([outs[p, h, c] for c in range(len(chunks))], axis=0)
        o = o * lax.rsqrt(jnp.mean(o * o, axis=-1, keepdims=True) + RMS_EPS) * gnorm
        mix_scr[rows(p), hs] = (o * _silu(z_ref[rows(p), hs].astype(F32))).astype(mix_scr.dtype)

    tok = (t * tt + 1 + lax.broadcasted_iota(jnp.int32, (tt, POOL_GROUP_W), 0)).astype(F32)
    pscale = pscale_ref[...]
    for gi, win in enumerate(POOL_WINDOWS):
        cs = slice(gi * POOL_GROUP_W, (gi + 1) * POOL_GROUP_W)
        cur = p_ext[POOL_HALO:POOL_HALO + tt, cs]
        tot = cur
        for back in range(1, win):
            tot = tot + p_ext[pl.ds(POOL_HALO - back, tt), cs]
        pooled = tot / jnp.minimum(tok, float(win)) - cur
        y = _mm(pooled.astype(BF16), poolw_ref[gi])
        mix_scr[:, GDN_V_W + gi * POOL_GROUP_W:GDN_V_W + (gi + 1) * POOL_GROUP_W] = (
            y * pscale[:, cs]).astype(mix_scr.dtype)

    p_ext[0:POOL_HALO, :] = p_ext[tt:tt + POOL_HALO, :]

    gate = mod_ref[...][2:3]
    o_ref[...] = x_ref[...] + gate * _mm(mix_scr[...], wout_ref[...])


def _gdn_pool(qkv, z, p, ba, alog_row, dtb_row, gnorm, pool_w, pool_scale, w_out, layer, x, mod):
    b, t, d = x.shape
    tt = GDN_TILE
    row = lambda width: pl.BlockSpec((None, tt, width), lambda bi, i: (bi, i, 0))
    full = lambda a: pl.BlockSpec(a.shape, lambda bi, i: (0,) * a.ndim)
    small = (alog_row, dtb_row, gnorm, pool_w, pool_scale)
    return pl.pallas_call(
        _gdn_pool_kernel,
        grid=(b, t // tt),
        in_specs=[row(GDN_QKV_W), row(GDN_V_W), row(POOL_W), row(LANES)] + [full(a) for a in small] + [
            _resident(w_out, layer), row(d), pl.BlockSpec((None, 6, d), lambda bi, i: (bi, 0, 0))],
        out_specs=row(d),
        out_shape=jax.ShapeDtypeStruct((b, t, d), F32),
        scratch_shapes=[
            pltpu.VMEM((tt + POOL_HALO, POOL_W), F32),
            pltpu.VMEM((GDN_HEADS, GDN_DK, GDN_DV), F32),
            pltpu.VMEM((tt, GDN_V_W + POOL_W), BF16),
        ],
        compiler_params=_params(("parallel", "arbitrary")),
        name="gdn_pool_out_proj",
    )(qkv, z, p, ba, *small, w_out, x, mod)


def _out_res_kernel(a_ref, w_ref, x_ref, mod_ref, o_ref, *, gate_row):
    gate = mod_ref[...][gate_row:gate_row + 1]
    o_ref[...] = x_ref[...] + gate * _mm(a_ref[...], w_ref[...])


def _out_res(a, w, layer, x, mod, gate_row):
    b, t, d = x.shape
    k = a.shape[-1]
    tm = ROW_TILE
    return pl.pallas_call(
        functools.partial(_out_res_kernel, gate_row=gate_row),
        grid=(b, t // tm),
        in_specs=[
            pl.BlockSpec((None, tm, k), lambda bi, i: (bi, i, 0)),
            _resident(w, layer),
            pl.BlockSpec((None, tm, d), lambda bi, i: (bi, i, 0)),
            pl.BlockSpec((None, 6, d), lambda bi, i: (bi, 0, 0)),
        ],
        out_specs=pl.BlockSpec((None, tm, d), lambda bi, i: (bi, i, 0)),
        out_shape=jax.ShapeDtypeStruct((b, t, d), F32),
        compiler_params=_params(("parallel", "parallel")),
        name="out_proj_residual",
    )(a, w, x, mod)


def _odd_in_kernel(x_ref, mod_ref, gain_ref, w_ref, qn_ref, kn_ref, o0_ref, o1_ref, o2_ref,
                   hn_slab, h_scr):
    tm = x_ref.shape[0]
    d = x_ref.shape[1]
    m = mod_ref[...]
    hn = _mod_norm(x_ref[...], gain_ref[...], m[0:1], m[1:2])
    for gi, (_, dil) in enumerate(DIL_PATTERNS):
        if dil == 1:
            h_scr[gi] = hn.astype(BF16)
    for sb in range(d // LANES):
        hn_slab[sb] = hn[:, sb * LANES:(sb + 1) * LANES]
    for gi, (_, dil) in enumerate(DIL_PATTERNS):
        per = tm // dil
        for r in range(dil if dil > 1 else 0):
            for sb in range(d // LANES):
                h_scr[gi, r * per:(r + 1) * per, sb * LANES:(sb + 1) * LANES] = (
                    hn_slab[sb, pl.ds(r, per, stride=dil), :].astype(BF16))
    qn = qn_ref[...] * (ATT_DH ** -0.5 * LOG2_E)
    kn = kn_ref[...]
    half = ATT_W // 2
    for gi, o_ref in enumerate((o0_ref, o1_ref, o2_ref)):
        dil = DIL_PATTERNS[gi][1]
        per = tm // dil
        for blk in range(3 * ATT_W // half):
            lo = blk * half
            y = _mm(h_scr[gi], w_ref[:, gi * 3 * ATT_W + lo:gi * 3 * ATT_W + lo + half])
            part = lo // ATT_W
            for h in range(half // ATT_DH):
                cs = slice(lo + h * ATT_DH, lo + (h + 1) * ATT_DH)
                yh = y[:, h * ATT_DH:(h + 1) * ATT_DH]
                if part < 2:
                    ms = jnp.mean(yh * yh, axis=-1, keepdims=True)
                    yh = yh * lax.rsqrt(ms + RMS_EPS) * (qn if part == 0 else kn)
                yh = yh.astype(o_ref.dtype)
                for r in range(dil):
                    o_ref[r, :, cs] = yh[r * per:(r + 1) * per]


def _odd_in(x, mod, gain, w, layer, q_norm, k_norm):
    b, t, d = x.shape
    tm = ROW_TILE
    dils = [dl for _, dl in DIL_PATTERNS]
    return pl.pallas_call(
        _odd_in_kernel,
        grid=(b, t // tm),
        in_specs=[
            pl.BlockSpec((None, tm, d), lambda bi, i: (bi, i, 0)),
            pl.BlockSpec((None, 6, d), lambda bi, i: (bi, 0, 0)),
            pl.BlockSpec((1, d), lambda bi, i: (0, 0)),
            _resident(w, layer),
            pl.BlockSpec((1, ATT_DH), lambda bi, i: (0, 0)),
            pl.BlockSpec((1, ATT_DH), lambda bi, i: (0, 0)),
        ],
        out_specs=[pl.BlockSpec((None, dl, tm // dl, 3 * ATT_W), lambda bi, i: (bi, 0, i, 0))
                   for dl in dils],
        out_shape=[jax.ShapeDtypeStruct((b, dl, t // dl, 3 * ATT_W), BF16) for dl in dils],
        scratch_shapes=[pltpu.VMEM((d // LANES, tm, LANES), F32),
                        pltpu.VMEM((len(dils), tm, d), BF16)],
        compiler_params=_params(("parallel", "parallel")),
        name="odd_in_proj",
    )(x, mod, gain, w, q_norm, k_norm)


def _attn_kernel(q_ref, kp_ref, kc_ref, vp_ref, vc_ref, o_ref, stat_ref, bias_scr, *, dil, n_back):
    blk = ATT_BLOCK
    first_step = jnp.logical_and(jnp.logical_and(pl.program_id(0) == 0, pl.program_id(1) == 0),
                                 pl.program_id(2) == 0)

    @pl.when(first_step)
    def _():
        a = lax.broadcasted_iota(jnp.int32, (blk, 2 * blk), 0)
        j = lax.broadcasted_iota(jnp.int32, (blk, 2 * blk), 1)
        rel = blk + a - j
        in_band = jnp.logical_and(rel >= 0, rel <= n_back)
        own = jnp.logical_and(in_band, j >= blk)
        relf = rel.astype(F32)
        for h in range(ATT_HEADS):
            bias = -(ATT_SLOPES[h] * dil * LOG2_E) * relf
            bias_scr[0, h] = jnp.where(own, bias, MASKED_SCORE)
            bias_scr[1, h] = jnp.where(in_band, bias, MASKED_SCORE)

    heads = [slice(h * ATT_DH, (h + 1) * ATT_DH) for h in range(ATT_HEADS)]
    lane = lax.broadcasted_iota(jnp.int32, (blk, LANES), 1)
    for i in range(q_ref.shape[0] // blk):
        rows = slice(i * blk, (i + 1) * blk)
        if i == 0:
            variant = jnp.minimum(pl.program_id(2), 1)
            keys = lambda hs: jnp.concatenate([kp_ref[:, hs], kc_ref[rows, hs]], axis=0)
            vals = lambda hs: jnp.concatenate([vp_ref[:, hs], vc_ref[rows, hs]], axis=0)
        else:
            variant = 1
            both = slice((i - 1) * blk, (i + 1) * blk)
            keys = lambda hs, both=both: kc_ref[both, hs]
            vals = lambda hs, both=both: vc_ref[both, hs]
        s = [_mm_nt(q_ref[rows, hs], keys(hs)) + bias_scr[variant, h] for h, hs in enumerate(heads)]
        m = [jnp.max(sh, axis=-1, keepdims=True) for sh in s]
        p = [jnp.exp2(sh - mh) for sh, mh in zip(s, m)]
        l = [jnp.sum(ph, axis=-1, keepdims=True) for ph in p]
        o = [_mm(ph.astype(BF16), vals(hs)) for ph, hs in zip(p, heads)]
        stats = jnp.zeros((blk, LANES), F32)
        for h, hs in enumerate(heads):
            o_ref[rows, hs] = o[h].astype(o_ref.dtype)
            stats = jnp.where(lane == h, m[h], stats)
            stats = jnp.where(lane == ATT_HEADS + h, l[h], stats)
        stat_ref[rows, :] = stats


def _attention_group(qkv, window, dil):
    b, _, length, _ = qkv.shape
    qblocks = min(ATT_QBLOCKS, length // ATT_BLOCK)
    rows = qblocks * ATT_BLOCK
    assert length % rows == 0

    def cur(part):
        return pl.BlockSpec((None, None, rows, ATT_W), lambda bi, r, n: (bi, r, n, part))

    def prev(part):
        return pl.BlockSpec((None, None, ATT_BLOCK, ATT_W),
                            lambda bi, r, n: (bi, r, jnp.maximum(n * qblocks - 1, 0), part))

    return pl.pallas_call(
        functools.partial(_attn_kernel, dil=dil, n_back=window // dil),
        grid=(b, dil, length // rows),
        in_specs=[cur(0), prev(1), cur(1), prev(2), cur(2)],
        out_specs=[
            pl.BlockSpec((None, None, rows, ATT_W), lambda bi, r, n: (bi, r, n, 0)),
            pl.BlockSpec((None, None, rows, LANES), lambda bi, r, n: (bi, r, n, 0)),
        ],
        out_shape=[
            jax.ShapeDtypeStruct((b, dil, length, ATT_W), BF16),
            jax.ShapeDtypeStruct((b, dil, length, LANES), F32),
        ],
        scratch_shapes=[pltpu.VMEM((2, ATT_HEADS, ATT_BLOCK, 2 * ATT_BLOCK), F32)],
        compiler_params=_params(("arbitrary", "arbitrary", "arbitrary")),
        name=f"dilated_attention_d{dil}",
    )(qkv, qkv, qkv, qkv, qkv)


def _merge_out_kernel(o0_ref, o1_ref, o2_ref, l0_ref, l1_ref, l2_ref, w_ref, x_ref, mod_ref,
                      out_ref, o_slab, l_slab, a_scr):
    tm = x_ref.shape[0]
    o_refs = (o0_ref, o1_ref, o2_ref)
    l_refs = (l0_ref, l1_ref, l2_ref)
    heads = [slice(h * ATT_DH, (h + 1) * ATT_DH) for h in range(ATT_HEADS)]
    for gi, (_, dil) in enumerate(DIL_PATTERNS):
        if dil == 1:
            continue
        for r in range(dil):
            dst = pl.ds(r, tm // dil, stride=dil)
            l_slab[gi, dst, :] = l_refs[gi][r]
            for h, hs in enumerate(heads):
                o_slab[gi, h, dst, :] = o_refs[gi][r, :, hs].astype(F32)

    def lse_of(gi):
        return l_refs[gi][0] if DIL_PATTERNS[gi][1] == 1 else l_slab[gi]

    def out_of(gi, h):
        if DIL_PATTERNS[gi][1] == 1:
            return o_refs[gi][0, :, heads[h]].astype(F32)
        return o_slab[gi, h]

    stats = [lse_of(gi) for gi in range(len(DIL_PATTERNS))]
    dens = [pltpu.roll(st, LANES - ATT_HEADS, axis=1) for st in stats]
    mx = jnp.maximum(jnp.maximum(stats[0], stats[1]), stats[2])
    es = [jnp.exp2(st - mx) for st in stats]
    inv = 1.0 / (es[0] * dens[0] + es[1] * dens[1] + es[2] * dens[2])
    wts = [e * inv for e in es]
    for h, hs in enumerate(heads):
        merged = (_lane_bcast(wts[0], h) * out_of(0, h) + _lane_bcast(wts[1], h) * out_of(1, h)
                  + _lane_bcast(wts[2], h) * out_of(2, h))
        a_scr[:, hs] = merged.astype(BF16)
    gate = mod_ref[...][2:3]
    out_ref[...] = x_ref[...] + gate * _mm(a_scr[...], w_ref[...])


def _merge_out(outs, lses, w, layer, x, mod):
    b, t, d = x.shape
    tm = ROW_TILE
    groups = len(DIL_PATTERNS)
    row = lambda width: pl.BlockSpec((None, tm, width), lambda bi, i: (bi, i, 0))
    by_residue = lambda dl, width: pl.BlockSpec((None, dl, tm // dl, width),
                                                lambda bi, i: (bi, 0, i, 0))
    dils = [dl for _, dl in DIL_PATTERNS]
    return pl.pallas_call(
        _merge_out_kernel,
        grid=(b, t // tm),
        in_specs=[by_residue(dl, ATT_W) for dl in dils] + [by_residue(dl, LANES) for dl in dils] + [
            _resident(w, layer), row(d), pl.BlockSpec((None, 6, d), lambda bi, i: (bi, 0, 0))],
        out_specs=row(d),
        out_shape=jax.ShapeDtypeStruct((b, t, d), F32),
        scratch_shapes=[
            pltpu.VMEM((groups, ATT_HEADS, tm, ATT_DH), F32),
            pltpu.VMEM((groups, tm, LANES), F32),
            pltpu.VMEM((tm, ATT_W), BF16),
        ],
        compiler_params=_params(("parallel", "parallel")),
        name="attn_merge_out_proj",
    )(*outs, *lses, w, x, mod)


def _ffn_up_kernel(x_ref, xh_ref, mod_ref, gain_ref, wg_ref, wv_ref, cw_ref, cb_ref, o_ref,
                   h_scr, g_scr, a_scr, *, chunk):
    tm = x_ref.shape[0]
    halo = FFN_HALO
    m = mod_ref[...]
    gain = gain_ref[...]
    shift, scale = m[3:4], m[4:5]
    h_scr[halo:, :] = _mod_norm(x_ref[...], gain, shift, scale).astype(BF16)
    live = jnp.where(pl.program_id(1) > 0, 1.0, 0.0)
    h_scr[0:halo, :] = (_mod_norm(xh_ref[...], gain, shift, scale) * live).astype(BF16)
    cw = cw_ref[...]
    cb = cb_ref[...]
    n_chunks = o_ref.shape[-1] // chunk
    for c in range(n_chunks):
        cs = slice(c * chunk, (c + 1) * chunk)
        g_scr[...] = _mm(h_scr[...], wg_ref[:, cs])
        gate = cb[:, cs] + cw[FFN_CONV - 1:FFN_CONV, cs] * g_scr[halo:, :]
        for j in range(FFN_CONV - 1):
            gate = gate + cw[j:j + 1, cs] * g_scr[pl.ds(halo - (FFN_CONV - 1) + j, tm), :]
        a_scr[:, cs] = _silu(gate)
    for c in range(n_chunks):
        cs = slice(c * chunk, (c + 1) * chunk)
        o_ref[:, cs] = (a_scr[:, cs] * _mm(h_scr[halo:, :], wv_ref[:, cs])).astype(o_ref.dtype)


def _ffn_up(x, mod, gain, w_up, layer, conv_w, conv_b):
    b, t, d = x.shape
    f = w_up.shape[-1] // 2
    tm = ROW_TILE
    chunk = 256
    per = tm // FFN_HALO
    half = lambda j: _resident(w_up, layer, col_blocks=2, col=j)
    return pl.pallas_call(
        functools.partial(_ffn_up_kernel, chunk=chunk),
        grid=(b, t // tm),
        in_specs=[
            pl.BlockSpec((None, tm, d), lambda bi, i: (bi, i, 0)),
            pl.BlockSpec((None, FFN_HALO, d), lambda bi, i: (bi, jnp.maximum(i * per - 1, 0), 0)),
            pl.BlockSpec((None, 6, d), lambda bi, i: (bi, 0, 0)),
            pl.BlockSpec((1, d), lambda bi, i: (0, 0)),
            half(0),
            half(1),
            pl.BlockSpec(conv_w.shape, lambda bi, i: (0, 0)),
            pl.BlockSpec(conv_b.shape, lambda bi, i: (0, 0)),
        ],
        out_specs=pl.BlockSpec((None, tm, f), lambda bi, i: (bi, i, 0)),
        out_shape=jax.ShapeDtypeStruct((b, t, f), BF16),
        scratch_shapes=[pltpu.VMEM((tm + FFN_HALO, d), BF16), pltpu.VMEM((tm + FFN_HALO, chunk), F32),
                        pltpu.VMEM((tm, f), F32)],
        compiler_params=_params(("parallel", "parallel")),
        name="ffn_up_conv_gate",
    )(x, x, mod, gain, w_up, w_up, conv_w, conv_b)


def _lane_row(values, offset):
    return jnp.zeros((1, LANES), F32).at[0, offset:offset + values.shape[0]].set(values)


def kernel(x, c, ada_w, ada_b, norm_mix, norm_ffn, ev_w_in, ev_w_out, gdn_conv_w, gdn_a_log,
           gdn_dt_bias, gdn_norm, pool_w, pool_scale, od_w_in, od_w_out, att_q_norm, att_k_norm,
           ffn_w_up, ffn_conv_w, ffn_conv_b, ffn_w_down):
    b, t, d = x.shape
    depth = ada_w.shape[0]
    assert t % (ATT_BLOCK * max(dl for _, dl in DIL_PATTERNS)) == 0 and t % ROW_TILE == 0
    d_ff = ffn_w_down.shape[1]

    mod = _ada(c, ada_w, ada_b)

    i1 = GDN_QKV_W + GDN_V_W
    i2 = i1 + 2 * GDN_HEADS
    pad = jnp.zeros(ev_w_in.shape[:2] + (LANES - 2 * GDN_HEADS,), ev_w_in.dtype)
    ev_w = jnp.concatenate([ev_w_in[..., :i1], ev_w_in[..., i2:], ev_w_in[..., i1:i2], pad],
                           axis=-1).astype(BF16)
    ev_wo, od_w, od_wo = ev_w_out.astype(BF16), od_w_in.astype(BF16), od_w_out.astype(BF16)
    up_w, down_w = ffn_w_up.astype(BF16), ffn_w_down.astype(BF16)

    for i in range(depth):
        m = mod[i]
        gain_m = norm_mix[i].reshape(1, d)
        if i % 2 == 0:
            e = i // 2
            qkv, z, p, ba = _even_in(x, m, gain_m, ev_w, e, gdn_conv_w[e])
            x = _gdn_pool(
                qkv, z, p, ba,
                _lane_row(gdn_a_log[e], GDN_HEADS), _lane_row(gdn_dt_bias[e], GDN_HEADS),
                gdn_norm[e].reshape(1, GDN_DV), pool_w[e].astype(BF16),
                pool_scale[e].reshape(1, POOL_W), ev_wo, e, x, m)
        else:
            o = i // 2
            qkvs = _odd_in(x, m, gain_m, od_w, o,
                           att_q_norm[o].reshape(1, ATT_DH), att_k_norm[o].reshape(1, ATT_DH))
            outs, lses = [], []
            for qkv, (window, dil) in zip(qkvs, DIL_PATTERNS):
                og, lg = _attention_group(qkv, window, dil)
                outs.append(og)
                lses.append(lg)
            x = _merge_out(outs, lses, od_wo, o, x, m)

        act = _ffn_up(x, m, norm_ffn[i].reshape(1, d), up_w, i,
                      ffn_conv_w[i], ffn_conv_b[i].reshape(1, d_ff))
        x = _out_res(act, down_w, i, x, m, 5)
    return x
```

```python
import functools

import jax
import jax.numpy as jnp
from jax import lax
from jax.experimental import pallas as pl
from jax.experimental.pallas import tpu as pltpu

F32 = jnp.float32
BF16 = jnp.bfloat16

RMS_EPS = 1e-6

GDN_HEADS = 4
GDN_DK = 128
GDN_DV = 128
GDN_QK_W = GDN_HEADS * GDN_DK
GDN_V_W = GDN_HEADS * GDN_DV
GDN_QKV_W = 2 * GDN_QK_W + GDN_V_W
GDN_CONV = 4
GDN_CHUNK = 64
GDN_PAIR = 2 * GDN_CHUNK
GDN_TILE = 4 * GDN_PAIR
GDN_INV_BLOCK = 16

POOL_WINDOWS = (2, 4, 8, 16)
POOL_GROUP_W = 128
POOL_W = len(POOL_WINDOWS) * POOL_GROUP_W
POOL_HALO = 16

DIL_PATTERNS = ((128, 1), (512, 4), (2048, 16))
ATT_HEADS = 8
ATT_DH = 128
ATT_W = ATT_HEADS * ATT_DH
ATT_BLOCK = 128
ATT_QBLOCKS = 8
ATT_SLOPES = tuple(2.0 ** (-8.0 * (h + 1) / ATT_HEADS) for h in range(ATT_HEADS))
MASKED_SCORE = -1e30
LOG2_E = 1.4426950408889634
LN_2 = 0.6931471805599453

FFN_CONV = 3
FFN_HALO = 16
EVEN_HALO = 16
LANES = 128
ROW_TILE = 512
VMEM_LIMIT = 56 * 1024 * 1024


def _sigmoid(x):
    return 1.0 / (1.0 + jnp.exp(-x))


def _silu(x):
    return x * _sigmoid(x)


def _softplus(x):
    return jnp.maximum(x, 0.0) + jnp.log(1.0 + jnp.exp(-jnp.abs(x)))


def _mm(a, b):
    return jnp.dot(a, b, preferred_element_type=F32)


def _mm_nt(a, b):
    return lax.dot_general(a, b, (((1,), (1,)), ((), ())), preferred_element_type=F32)


def _mod_norm(x, gain, shift, scale):
    ms = jnp.mean(x * x, axis=-1, keepdims=True)
    return x * lax.rsqrt(ms + RMS_EPS) * (gain * (1.0 + scale)) + shift


def _params(semantics):
    return pltpu.CompilerParams(dimension_semantics=semantics, vmem_limit_bytes=VMEM_LIMIT)


def _resident(stack, layer, col_blocks=1, col=0):
    _, k, n = stack.shape
    return pl.BlockSpec((None, k, n // col_blocks), lambda *_: (layer, 0, col),
                        pipeline_mode=pl.Buffered(1))


def _ada_kernel(c_ref, w_ref, b_ref, o_ref):
    cs = _silu(c_ref[...])
    o_ref[...] = _mm(cs.astype(BF16), w_ref[...].astype(BF16)) + b_ref[...]


def _ada(c, ada_w, ada_b):
    depth, d, n = ada_w.shape
    b = c.shape[0]
    rows = 8
    cp = jnp.pad(c, ((0, rows - b), (0, 0)))
    tn = 1536
    out = pl.pallas_call(
        _ada_kernel,
        grid=(depth, n // tn),
        in_specs=[
            pl.BlockSpec((rows, d), lambda l, j: (0, 0)),
            pl.BlockSpec((None, d, tn), lambda l, j: (l, 0, j)),
            pl.BlockSpec((None, 1, tn), lambda l, j: (l, 0, j)),
        ],
        out_specs=pl.BlockSpec((None, rows, tn), lambda l, j: (l, 0, j)),
        out_shape=jax.ShapeDtypeStruct((depth, rows, n), F32),
        compiler_params=_params(("parallel", "parallel")),
        name="ada_mod",
    )(cp, ada_w, ada_b.reshape(depth, 1, n))
    return out[:, :b].reshape(depth, b, 6, d)


def _even_in_kernel(x_ref, xh_ref, mod_ref, gain_ref, w_ref, cw_ref, qkv_ref, z_ref, p_ref, ba_ref,
                    h_scr, g_scr):
    tm = x_ref.shape[0]
    halo = EVEN_HALO
    m = mod_ref[...]
    gain = gain_ref[...]
    shift, scale = m[0:1], m[1:2]
    h_scr[halo:, :] = _mod_norm(x_ref[...], gain, shift, scale).astype(BF16)
    live = jnp.where(pl.program_id(1) > 0, 1.0, 0.0)
    h_scr[0:halo, :] = (_mod_norm(xh_ref[...], gain, shift, scale) * live).astype(BF16)

    cw = cw_ref[...]
    width = g_scr.shape[-1]
    for lo in range(0, GDN_QKV_W, width):
        cs = slice(lo, lo + width)
        part = lo // GDN_QK_W
        g_scr[...] = _mm(h_scr[...], w_ref[:, cs])
        acc = cw[GDN_CONV - 1:GDN_CONV, cs] * g_scr[halo:, :]
        for j in range(GDN_CONV - 1):
            acc = acc + cw[j:j + 1, cs] * g_scr[pl.ds(halo - (GDN_CONV - 1) + j, tm), :]
        act = _silu(acc)
        for h in range(width // GDN_DK):
            a = act[:, h * GDN_DK:(h + 1) * GDN_DK]
            if part == 0:
                a = _l2norm(a) * (GDN_DK ** -0.5)
            elif part == 1:
                a = _l2norm(a)
            qkv_ref[:, lo + h * GDN_DK:lo + (h + 1) * GDN_DK] = a.astype(qkv_ref.dtype)

    col = GDN_QKV_W
    for ref in (z_ref, p_ref, ba_ref):
        n = ref.shape[-1]
        ref[...] = _mm(h_scr[halo:, :], w_ref[:, col:col + n]).astype(ref.dtype)
        col += n


def _even_in(x, mod, gain, w, layer, conv_w):
    b, t, d = x.shape
    tm = ROW_TILE
    per = tm // EVEN_HALO
    row = lambda width: pl.BlockSpec((None, tm, width), lambda bi, i: (bi, i, 0))
    return pl.pallas_call(
        _even_in_kernel,
        grid=(b, t // tm),
        in_specs=[
            row(d),
            pl.BlockSpec((None, EVEN_HALO, d), lambda bi, i: (bi, jnp.maximum(i * per - 1, 0), 0)),
            pl.BlockSpec((None, 6, d), lambda bi, i: (bi, 0, 0)),
            pl.BlockSpec((1, d), lambda bi, i: (0, 0)),
            _resident(w, layer),
            pl.BlockSpec(conv_w.shape, lambda bi, i: (0, 0)),
        ],
        out_specs=[row(GDN_QKV_W), row(GDN_V_W), row(POOL_W), row(LANES)],
        out_shape=[
            jax.ShapeDtypeStruct((b, t, GDN_QKV_W), BF16),
            jax.ShapeDtypeStruct((b, t, GDN_V_W), BF16),
            jax.ShapeDtypeStruct((b, t, POOL_W), BF16),
            jax.ShapeDtypeStruct((b, t, LANES), F32),
        ],
        scratch_shapes=[pltpu.VMEM((tm + EVEN_HALO, d), BF16),
                        pltpu.VMEM((tm + EVEN_HALO, 2 * GDN_DK), F32)],
        compiler_params=_params(("parallel", "parallel")),
        name="even_in_proj",
    )(x, x, mod, gain, w, conv_w)


def _lane_bcast(x, lane):
    return jnp.broadcast_to(x[:, lane:lane + 1], (x.shape[0], LANES))


def _row_bcast(x, row, rows):
    return jnp.broadcast_to(x[row:row + 1, :], (rows, x.shape[1]))


def _l2norm(x):
    return x * lax.rsqrt(jnp.sum(x * x, axis=-1, keepdims=True) + RMS_EPS)


def _unit_lower_solve(lows, rhss, blk, eye):
    bf = lambda xs: [x.astype(BF16) for x in xs]
    mm = lambda xs, ys: [_mm(x, y) for x, y in zip(xs, ys)]
    d = [jnp.where(blk, low, 0.0) for low in lows]
    n = bf([low - di for low, di in zip(lows, d)])
    db = bf(d)
    d2 = mm(db, db)
    d2b = bf(d2)
    d4 = mm(d2b, d2b)
    d4b = bf(d4)
    d8 = mm(d4b, d4b)
    acc = mm(bf([eye - x for x in d]), bf([eye + x for x in d2]))
    acc = mm(bf(acc), bf([eye + x for x in d4]))
    dinv = bf(mm(bf(acc), bf([eye + x for x in d8])))
    m = mm(dinv, n)
    mb = bf(m)
    m2 = mm(mb, mb)
    t1 = bf(mm(dinv, rhss))
    t2 = bf(mm(bf([eye + x for x in m2]), t1))
    return mm(bf([eye - x for x in m]), t2)


def _gdn_pool_kernel(qkv_ref, z_ref, p_ref, ba_ref, alog_ref, dtb_ref, gnorm_ref,
                     poolw_ref, pscale_ref, wout_ref, x_ref, mod_ref, o_ref, p_ext, s_scr, mix_scr):
    tt = GDN_TILE
    ch = GDN_CHUNK
    t = pl.program_id(1)

    @pl.when(t == 0)
    def _():
        p_ext[0:POOL_HALO, :] = jnp.zeros((POOL_HALO, POOL_W), F32)
        s_scr[...] = jnp.zeros(s_scr.shape, F32)

    p_ext[POOL_HALO:POOL_HALO + tt, :] = p_ref[...].astype(F32)

    pt = GDN_PAIR
    row = lax.broadcasted_iota(jnp.int32, (pt, pt), 0)
    col = lax.broadcasted_iota(jnp.int32, (pt, pt), 1)
    chunk_of = lambda idx: jnp.right_shift(idx, ch.bit_length() - 1)
    block_of = lambda idx: jnp.right_shift(idx, GDN_INV_BLOCK.bit_length() - 1)
    same = chunk_of(row) == chunk_of(col)
    causal = jnp.logical_and(same, row >= col)
    strict = jnp.logical_and(same, row > col)
    blk = block_of(row) == block_of(col)
    eye = jnp.where(row == col, 1.0, 0.0).astype(F32)
    first = row < ch
    tri = jnp.where(causal, 1.0, 0.0).astype(F32)

    pairs = tt // pt
    units = [(p, h) for p in range(pairs) for h in range(GDN_HEADS)]
    rows = lambda p: slice(p * pt, (p + 1) * pt)
    per_unit = lambda fn: [fn(p, h) for p, h in units]
    bf = lambda xs: [x.astype(BF16) for x in xs]

    beta_all, gc_all, gc_all_t = [], [], []
    for p in range(pairs):
        ba = ba_ref[rows(p), :]
        beta_all.append(_sigmoid(ba))
        g = -jnp.exp(alog_ref[...]) * _softplus(ba + dtb_ref[...])
        gc = jnp.dot(tri, g, preferred_element_type=F32, precision=lax.Precision.HIGHEST)
        gc_all.append(gc)
        gc_all_t.append(gc.T)

    q16 = per_unit(lambda p, h: qkv_ref[rows(p), h * GDN_DK:(h + 1) * GDN_DK])
    k16 = per_unit(lambda p, h: qkv_ref[rows(p), GDN_QK_W + h * GDN_DK:GDN_QK_W + (h + 1) * GDN_DK])
    v16 = per_unit(lambda p, h: qkv_ref[rows(p),
                                        2 * GDN_QK_W + h * GDN_DV:2 * GDN_QK_W + (h + 1) * GDN_DV])
    q = [x.astype(F32) for x in q16]
    k = [x.astype(F32) for x in k16]
    v = [x.astype(F32) for x in v16]
    beta = per_unit(lambda p, h: _lane_bcast(beta_all[p], h))
    gcol = per_unit(lambda p, h: _lane_bcast(gc_all[p], GDN_HEADS + h))
    grow = per_unit(lambda p, h: _row_bcast(gc_all_t[p], GDN_HEADS + h, pt))
    decay = [jnp.where(causal, jnp.exp(jnp.where(causal, gi - gj, 0.0)), 0.0)
             for gi, gj in zip(gcol, grow)]
    egc = [jnp.exp(g) for g in gcol]
    kb = [ki * bi for ki, bi in zip(k, beta)]
    low = [jnp.where(strict, _mm_nt(a, b) * d, 0.0) for a, b, d in zip(bf(kb), k16, decay)]
    intra = bf([_mm_nt(a, b) * d for a, b, d in zip(q16, k16, decay)])
    rhs = bf([jnp.concatenate([vi * bi, kbi * ei], axis=1)
              for vi, bi, kbi, ei in zip(v, beta, kb, egc)])
    uw = bf(_unit_lower_solve(low, rhs, blk, eye))

    iu = [_mm(a, b) for a, b in zip(intra, uw)]
    o_const = [x[:, :GDN_DV] for x in iu]
    q_eff = bf([qi * ei - x[:, GDN_DV:] for qi, ei, x in zip(q, egc, iu)])
    glast = [jnp.where(first, _row_bcast(g, ch - 1, pt), _row_bcast(g, pt - 1, pt)) for g in gcol]
    kdec = [ki * jnp.exp(gl - g) for ki, gl, g in zip(k, glast, gcol)]
    chunks = [slice(c * ch, (c + 1) * ch) for c in range(pt // ch)]
    ku = [[_mm(kd[rs].T.astype(BF16), x[rs]) for rs in chunks] for kd, x in zip(kdec, uw)]

    s = [s_scr[h] for h in range(GDN_HEADS)]
    outs = {}
    for p in range(pairs):
        for c, rs in enumerate(chunks):
            for h in range(GDN_HEADS):
                i = p * GDN_HEADS + h
                sb = s[h].astype(BF16)
                outs[p, h, c] = _mm(q_eff[i][rs], sb) + o_const[i][rs]
                gl = _row_bcast(egc[i], rs.stop - 1, GDN_DK)
                s[h] = gl * s[h] + ku[i][c][:, :GDN_DV] - _mm(ku[i][c][:, GDN_DV:].astype(BF16), sb)
    for h in range(GDN_HEADS):
        s_scr[h] = s[h]

    gnorm = gnorm_ref[...]
    for p, h in units:
        hs = slice(h * GDN_DV, (h + 1) * GDN_DV)
        o = jnp.concatenate([outs[p, h, c] for c in range(len(chunks))], axis=0)
        o = o * lax.rsqrt(jnp.mean(o * o, axis=-1, keepdims=True) + RMS_EPS) * gnorm
        mix_scr[rows(p), hs] = (o * _silu(z_ref[rows(p), hs].astype(F32))).astype(mix_scr.dtype)

    tok = (t * tt + 1 + lax.broadcasted_iota(jnp.int32, (tt, POOL_GROUP_W), 0)).astype(F32)
    pscale = pscale_ref[...]
    for gi, win in enumerate(POOL_WINDOWS):
        cs = slice(gi * POOL_GROUP_W, (gi + 1) * POOL_GROUP_W)
        cur = p_ext[POOL_HALO:POOL_HALO + tt, cs]
        tot = cur
        for back in range(1, win):
            tot = tot + p_ext[pl.ds(POOL_HALO - back, tt), cs]
        pooled = tot / jnp.minimum(tok, float(win)) - cur
        y = _mm(pooled.astype(BF16), poolw_ref[gi])
        mix_scr[:, GDN_V_W + gi * POOL_GROUP_W:GDN_V_W + (gi + 1) * POOL_GROUP_W] = (
            y * pscale[:, cs]).astype(mix_scr.dtype)

    p_ext[0:POOL_HALO, :] = p_ext[tt:tt + POOL_HALO, :]

    gate = mod_ref[...][2:3]
    o_ref[...] = x_ref[...] + gate * _mm(mix_scr[...], wout_ref[...])


def _gdn_pool(qkv, z, p, ba, alog_row, dtb_row, gnorm, pool_w, pool_scale, w_out, layer, x, mod):
    b, t, d = x.shape
    tt = GDN_TILE
    row = lambda width: pl.BlockSpec((None, tt, width), lambda bi, i: (bi, i, 0))
    full = lambda a: pl.BlockSpec(a.shape, lambda bi, i: (0,) * a.ndim)
    small = (alog_row, dtb_row, gnorm, pool_w, pool_scale)
    return pl.pallas_call(
        _gdn_pool_kernel,
        grid=(b, t // tt),
        in_specs=[row(GDN_QKV_W), row(GDN_V_W), row(POOL_W), row(LANES)] + [full(a) for a in small] + [
            _resident(w_out, layer), row(d), pl.BlockSpec((None, 6, d), lambda bi, i: (bi, 0, 0))],
        out_specs=row(d),
        out_shape=jax.ShapeDtypeStruct((b, t, d), F32),
        scratch_shapes=[
            pltpu.VMEM((tt + POOL_HALO, POOL_W), F32),
            pltpu.VMEM((GDN_HEADS, GDN_DK, GDN_DV), F32),
            pltpu.VMEM((tt, GDN_V_W + POOL_W), BF16),
        ],
        compiler_params=_params(("parallel", "arbitrary")),
        name="gdn_pool_out_proj",
    )(qkv, z, p, ba, *small, w_out, x, mod)


def _out_res_kernel(a_ref, w_ref, x_ref, mod_ref, o_ref, *, gate_row):
    gate = mod_ref[...][gate_row:gate_row + 1]
    o_ref[...] = x_ref[...] + gate * _mm(a_ref[...], w_ref[...])


def _out_res(a, w, layer, x, mod, gate_row):
    b, t, d = x.shape
    k = a.shape[-1]
    tm = 2 * ROW_TILE
    return pl.pallas_call(
        functools.partial(_out_res_kernel, gate_row=gate_row),
        grid=(b, t // tm),
        in_specs=[
            pl.BlockSpec((None, tm, k), lambda bi, i: (bi, i, 0)),
            _resident(w, layer),
            pl.BlockSpec((None, tm, d), lambda bi, i: (bi, i, 0)),
            pl.BlockSpec((None, 6, d), lambda bi, i: (bi, 0, 0)),
        ],
        out_specs=pl.BlockSpec((None, tm, d), lambda bi, i: (bi, i, 0)),
        out_shape=jax.ShapeDtypeStruct((b, t, d), F32),
        compiler_params=_params(("parallel", "parallel")),
        name="out_proj_residual",
    )(a, w, x, mod)


def _odd_in_kernel(x_ref, mod_ref, gain_ref, w_ref, qn_ref, kn_ref, o0_ref, o1_ref, o2_ref,
                   hn_slab, h_scr):
    tm = x_ref.shape[0]
    d = x_ref.shape[1]
    m = mod_ref[...]
    hn = _mod_norm(x_ref[...], gain_ref[...], m[0:1], m[1:2])
    for gi, (_, dil) in enumerate(DIL_PATTERNS):
        if dil == 1:
            h_scr[gi] = hn.astype(BF16)
    for sb in range(d // LANES):
        hn_slab[sb] = hn[:, sb * LANES:(sb + 1) * LANES]
    for gi, (_, dil) in enumerate(DIL_PATTERNS):
        per = tm // dil
        for r in range(dil if dil > 1 else 0):
            for sb in range(d // LANES):
                h_scr[gi, r * per:(r + 1) * per, sb * LANES:(sb + 1) * LANES] = (
                    hn_slab[sb, pl.ds(r, per, stride=dil), :].astype(BF16))
    qn = qn_ref[...] * (ATT_DH ** -0.5 * LOG2_E)
    kn = kn_ref[...]
    half = ATT_W // 2
    for gi, o_ref in enumerate((o0_ref, o1_ref, o2_ref)):
        dil = DIL_PATTERNS[gi][1]
        per = tm // dil
        for blk in range(3 * ATT_W // half):
            lo = blk * half
            y = _mm(h_scr[gi], w_ref[:, gi * 3 * ATT_W + lo:gi * 3 * ATT_W + lo + half])
            part = lo // ATT_W
            for h in range(half // ATT_DH):
                cs = slice(lo + h * ATT_DH, lo + (h + 1) * ATT_DH)
                yh = y[:, h * ATT_DH:(h + 1) * ATT_DH]
                if part < 2:
                    ms = jnp.mean(yh * yh, axis=-1, keepdims=True)
                    yh = yh * lax.rsqrt(ms + RMS_EPS) * (qn if part == 0 else kn)
                yh = yh.astype(o_ref.dtype)
                for r in range(dil):
                    o_ref[r, :, cs] = yh[r * per:(r + 1) * per]


def _odd_in(x, mod, gain, w, layer, q_norm, k_norm):
    b, t, d = x.shape
    tm = ROW_TILE
    dils = [dl for _, dl in DIL_PATTERNS]
    return pl.pallas_call(
        _odd_in_kernel,
        grid=(b, t // tm),
        in_specs=[
            pl.BlockSpec((None, tm, d), lambda bi, i: (bi, i, 0)),
            pl.BlockSpec((None, 6, d), lambda bi, i: (bi, 0, 0)),
            pl.BlockSpec((1, d), lambda bi, i: (0, 0)),
            _resident(w, layer),
            pl.BlockSpec((1, ATT_DH), lambda bi, i: (0, 0)),
            pl.BlockSpec((1, ATT_DH), lambda bi, i: (0, 0)),
        ],
        out_specs=[pl.BlockSpec((None, dl, tm // dl, 3 * ATT_W), lambda bi, i: (bi, 0, i, 0))
                   for dl in dils],
        out_shape=[jax.ShapeDtypeStruct((b, dl, t // dl, 3 * ATT_W), BF16) for dl in dils],
        scratch_shapes=[pltpu.VMEM((d // LANES, tm, LANES), F32),
                        pltpu.VMEM((len(dils), tm, d), BF16)],
        compiler_params=_params(("parallel", "parallel")),
        name="odd_in_proj",
    )(x, mod, gain, w, q_norm, k_norm)


def _attn_kernel(q_ref, kp_ref, kc_ref, vp_ref, vc_ref, o_ref, stat_ref, bias_scr, *, dil, n_back):
    blk = ATT_BLOCK
    first_step = jnp.logical_and(jnp.logical_and(pl.program_id(0) == 0, pl.program_id(1) == 0),
                                 pl.program_id(2) == 0)

    @pl.when(first_step)
    def _():
        a = lax.broadcasted_iota(jnp.int32, (blk, 2 * blk), 0)
        j = lax.broadcasted_iota(jnp.int32, (blk, 2 * blk), 1)
        rel = blk + a - j
        in_band = jnp.logical_and(rel >= 0, rel <= n_back)
        own = jnp.logical_and(in_band, j >= blk)
        relf = rel.astype(F32)
        for h in range(ATT_HEADS):
            bias = -(ATT_SLOPES[h] * dil * LOG2_E) * relf
            bias_scr[0, h] = jnp.where(own, bias, MASKED_SCORE)
            bias_scr[1, h] = jnp.where(in_band, bias, MASKED_SCORE)

    heads = [slice(h * ATT_DH, (h + 1) * ATT_DH) for h in range(ATT_HEADS)]
    lane = lax.broadcasted_iota(jnp.int32, (blk, LANES), 1)
    n_res, n_rows = q_ref.shape[0], q_ref.shape[1]
    for c, i in [(c, i) for c in range(n_res) for i in range(n_rows // blk)]:
        rows = slice(i * blk, (i + 1) * blk)
        if i == 0:
            variant = jnp.minimum(pl.program_id(2), 1)
            keys = lambda hs, c=c, rows=rows: jnp.concatenate([kp_ref[c, :, hs], kc_ref[c, rows, hs]],
                                                              axis=0)
            vals = lambda hs, c=c, rows=rows: jnp.concatenate([vp_ref[c, :, hs], vc_ref[c, rows, hs]],
                                                              axis=0)
        else:
            variant = 1
            both = slice((i - 1) * blk, (i + 1) * blk)
            keys = lambda hs, c=c, both=both: kc_ref[c, both, hs]
            vals = lambda hs, c=c, both=both: vc_ref[c, both, hs]
        s = [_mm_nt(q_ref[c, rows, hs], keys(hs)) + bias_scr[variant, h]
             for h, hs in enumerate(heads)]
        m = [jnp.max(sh, axis=-1, keepdims=True) for sh in s]
        p = [jnp.exp2(sh - mh) for sh, mh in zip(s, m)]
        l = [jnp.sum(ph, axis=-1, keepdims=True) for ph in p]
        o = [_mm(ph.astype(BF16), vals(hs)) for ph, hs in zip(p, heads)]
        stats = jnp.zeros((blk, LANES), F32)
        for h, hs in enumerate(heads):
            o_ref[c, rows, hs] = o[h].astype(o_ref.dtype)
            stats = jnp.where(lane == h, m[h], stats)
            stats = jnp.where(lane == ATT_HEADS + h, l[h], stats)
        stat_ref[c, rows, :] = stats


def _attention_group(qkv, window, dil):
    b, _, length, _ = qkv.shape
    qblocks = min(ATT_QBLOCKS, length // ATT_BLOCK)
    rows = qblocks * ATT_BLOCK
    n_res = min(dil, ATT_QBLOCKS // qblocks)
    assert length % rows == 0 and dil % n_res == 0

    def cur(part):
        return pl.BlockSpec((None, n_res, rows, ATT_W), lambda bi, r, n: (bi, r, n, part))

    def prev(part):
        return pl.BlockSpec((None, n_res, ATT_BLOCK, ATT_W),
                            lambda bi, r, n: (bi, r, jnp.maximum(n * qblocks - 1, 0), part))

    return pl.pallas_call(
        functools.partial(_attn_kernel, dil=dil, n_back=window // dil),
        grid=(b, dil // n_res, length // rows),
        in_specs=[cur(0), prev(1), cur(1), prev(2), cur(2)],
        out_specs=[
            pl.BlockSpec((None, n_res, rows, ATT_W), lambda bi, r, n: (bi, r, n, 0)),
            pl.BlockSpec((None, n_res, rows, LANES), lambda bi, r, n: (bi, r, n, 0)),
        ],
        out_shape=[
            jax.ShapeDtypeStruct((b, dil, length, ATT_W), BF16),
            jax.ShapeDtypeStruct((b, dil, length, LANES), F32),
        ],
        scratch_shapes=[pltpu.VMEM((2, ATT_HEADS, ATT_BLOCK, 2 * ATT_BLOCK), F32)],
        compiler_params=_params(("arbitrary", "arbitrary", "arbitrary")),
        name=f"dilated_attention_d{dil}",
    )(qkv, qkv, qkv, qkv, qkv)


def _merge_out_kernel(o0_ref, o1_ref, o2_ref, l0_ref, l1_ref, l2_ref, w_ref, x_ref, mod_ref,
                      out_ref, o_slab, l_slab, a_scr):
    tm = x_ref.shape[0]
    o_refs = (o0_ref, o1_ref, o2_ref)
    l_refs = (l0_ref, l1_ref, l2_ref)
    heads = [slice(h * ATT_DH, (h + 1) * ATT_DH) for h in range(ATT_HEADS)]
    for gi, (_, dil) in enumerate(DIL_PATTERNS):
        if dil == 1:
            continue
        for r in range(dil):
            dst = pl.ds(r, tm // dil, stride=dil)
            l_slab[gi, dst, :] = l_refs[gi][r]
            for h, hs in enumerate(heads):
                o_slab[gi, h, dst, :] = o_refs[gi][r, :, hs].astype(F32)

    def lse_of(gi):
        return l_refs[gi][0] if DIL_PATTERNS[gi][1] == 1 else l_slab[gi]

    def out_of(gi, h):
        if DIL_PATTERNS[gi][1] == 1:
            return o_refs[gi][0, :, heads[h]].astype(F32)
        return o_slab[gi, h]

    stats = [lse_of(gi) for gi in range(len(DIL_PATTERNS))]
    dens = [pltpu.roll(st, LANES - ATT_HEADS, axis=1) for st in stats]
    mx = jnp.maximum(jnp.maximum(stats[0], stats[1]), stats[2])
    es = [jnp.exp2(st - mx) for st in stats]
    inv = 1.0 / (es[0] * dens[0] + es[1] * dens[1] + es[2] * dens[2])
    wts = [e * inv for e in es]
    for h, hs in enumerate(heads):
        merged = (_lane_bcast(wts[0], h) * out_of(0, h) + _lane_bcast(wts[1], h) * out_of(1, h)
                  + _lane_bcast(wts[2], h) * out_of(2, h))
        a_scr[:, hs] = merged.astype(BF16)
    gate = mod_ref[...][2:3]
    out_ref[...] = x_ref[...] + gate * _mm(a_scr[...], w_ref[...])


def _merge_out(outs, lses, w, layer, x, mod):
    b, t, d = x.shape
    tm = 2 * ROW_TILE
    groups = len(DIL_PATTERNS)
    row = lambda width: pl.BlockSpec((None, tm, width), lambda bi, i: (bi, i, 0))
    by_residue = lambda dl, width: pl.BlockSpec((None, dl, tm // dl, width),
                                                lambda bi, i: (bi, 0, i, 0))
    dils = [dl for _, dl in DIL_PATTERNS]
    return pl.pallas_call(
        _merge_out_kernel,
        grid=(b, t // tm),
        in_specs=[by_residue(dl, ATT_W) for dl in dils] + [by_residue(dl, LANES) for dl in dils] + [
            _resident(w, layer), row(d), pl.BlockSpec((None, 6, d), lambda bi, i: (bi, 0, 0))],
        out_specs=row(d),
        out_shape=jax.ShapeDtypeStruct((b, t, d), F32),
        scratch_shapes=[
            pltpu.VMEM((groups, ATT_HEADS, tm, ATT_DH), F32),
            pltpu.VMEM((groups, tm, LANES), F32),
            pltpu.VMEM((tm, ATT_W), BF16),
        ],
        compiler_params=_params(("parallel", "parallel")),
        name="attn_merge_out_proj",
    )(*outs, *lses, w, x, mod)


def _ffn_up_kernel(x_ref, xh_ref, mod_ref, gain_ref, wg_ref, wv_ref, cw_ref, cb_ref, o_ref,
                   h_scr, g_scr, a_scr, *, chunk):
    tm = x_ref.shape[0]
    halo = FFN_HALO
    m = mod_ref[...]
    gain = gain_ref[...]
    shift, scale = m[3:4], m[4:5]
    h_scr[halo:, :] = _mod_norm(x_ref[...], gain, shift, scale).astype(BF16)
    live = jnp.where(pl.program_id(1) > 0, 1.0, 0.0)
    h_scr[0:halo, :] = (_mod_norm(xh_ref[...], gain, shift, scale) * live).astype(BF16)
    cw = cw_ref[...]
    cb = cb_ref[...]
    n_chunks = o_ref.shape[-1] // chunk
    for c in range(n_chunks):
        cs = slice(c * chunk, (c + 1) * chunk)
        g_scr[...] = _mm(h_scr[...], wg_ref[:, cs])
        gate = cb[:, cs] + cw[FFN_CONV - 1:FFN_CONV, cs] * g_scr[halo:, :]
        for j in range(FFN_CONV - 1):
            gate = gate + cw[j:j + 1, cs] * g_scr[pl.ds(halo - (FFN_CONV - 1) + j, tm), :]
        a_scr[:, cs] = _silu(gate)
    for c in range(n_chunks):
        cs = slice(c * chunk, (c + 1) * chunk)
        o_ref[:, cs] = (a_scr[:, cs] * _mm(h_scr[halo:, :], wv_ref[:, cs])).astype(o_ref.dtype)


def _ffn_up(x, mod, gain, w_up, layer, conv_w, conv_b):
    b, t, d = x.shape
    f = w_up.shape[-1] // 2
    tm = ROW_TILE
    chunk = 256
    per = tm // FFN_HALO
    half = lambda j: _resident(w_up, layer, col_blocks=2, col=j)
    return pl.pallas_call(
        functools.partial(_ffn_up_kernel, chunk=chunk),
        grid=(b, t // tm),
        in_specs=[
            pl.BlockSpec((None, tm, d), lambda bi, i: (bi, i, 0)),
            pl.BlockSpec((None, FFN_HALO, d), lambda bi, i: (bi, jnp.maximum(i * per - 1, 0), 0)),
            pl.BlockSpec((None, 6, d), lambda bi, i: (bi, 0, 0)),
            pl.BlockSpec((1, d), lambda bi, i: (0, 0)),
            half(0),
            half(1),
            pl.BlockSpec(conv_w.shape, lambda bi, i: (0, 0)),
            pl.BlockSpec(conv_b.shape, lambda bi, i: (0, 0)),
        ],
        out_specs=pl.BlockSpec((None, tm, f), lambda bi, i: (bi, i, 0)),
        out_shape=jax.ShapeDtypeStruct((b, t, f), BF16),
        scratch_shapes=[pltpu.VMEM((tm + FFN_HALO, d), BF16), pltpu.VMEM((tm + FFN_HALO, chunk), F32),
                        pltpu.VMEM((tm, f), F32)],
        compiler_params=_params(("parallel", "parallel")),
        name="ffn_up_conv_gate",
    )(x, x, mod, gain, w_up, w_up, conv_w, conv_b)


def _lane_row(values, offset):
    return jnp.zeros((1, LANES), F32).at[0, offset:offset + values.shape[0]].set(values)


def kernel(x, c, ada_w, ada_b, norm_mix, norm_ffn, ev_w_in, ev_w_out, gdn_conv_w, gdn_a_log,
           gdn_dt_bias, gdn_norm, pool_w, pool_scale, od_w_in, od_w_out, att_q_norm, att_k_norm,
           ffn_w_up, ffn_conv_w, ffn_conv_b, ffn_w_down):
    b, t, d = x.shape
    depth = ada_w.shape[0]
    assert t % (ATT_BLOCK * max(dl for _, dl in DIL_PATTERNS)) == 0 and t % (2 * ROW_TILE) == 0
    d_ff = ffn_w_down.shape[1]

    mod = _ada(c, ada_w, ada_b)

    i1 = GDN_QKV_W + GDN_V_W
    i2 = i1 + 2 * GDN_HEADS
    pad = jnp.zeros(ev_w_in.shape[:2] + (LANES - 2 * GDN_HEADS,), ev_w_in.dtype)
    ev_w = jnp.concatenate([ev_w_in[..., :i1], ev_w_in[..., i2:], ev_w_in[..., i1:i2], pad],
                           axis=-1).astype(BF16)
    ev_wo, od_w, od_wo = ev_w_out.astype(BF16), od_w_in.astype(BF16), od_w_out.astype(BF16)
    up_w, down_w = ffn_w_up.astype(BF16), ffn_w_down.astype(BF16)

    for i in range(depth):
        m = mod[i]
        gain_m = norm_mix[i].reshape(1, d)
        if i % 2 == 0:
            e = i // 2
            qkv, z, p, ba = _even_in(x, m, gain_m, ev_w, e, gdn_conv_w[e])
            x = _gdn_pool(
                qkv, z, p, ba,
                _lane_row(gdn_a_log[e], GDN_HEADS), _lane_row(gdn_dt_bias[e], GDN_HEADS),
                gdn_norm[e].reshape(1, GDN_DV), pool_w[e].astype(BF16),
                pool_scale[e].reshape(1, POOL_W), ev_wo, e, x, m)
        else:
            o = i // 2
            qkvs = _odd_in(x, m, gain_m, od_w, o,
                           att_q_norm[o].reshape(1, ATT_DH), att_k_norm[o].reshape(1, ATT_DH))
            outs, lses = [], []
            for qkv, (window, dil) in zip(qkvs, DIL_PATTERNS):
                og, lg = _attention_group(qkv, window, dil)
                outs.append(og)
                lses.append(lg)
            x = _merge_out(outs, lses, od_wo, o, x, m)

        act = _ffn_up(x, m, norm_ffn[i].reshape(1, d), up_w, i,
                      ffn_conv_w[i], ffn_conv_b[i].reshape(1, d_ff))
        x = _out_res(act, down_w, i, x, m, 5)
    return x
```

```python
import functools

import jax
import jax.numpy as jnp
from jax import lax
from jax.experimental import pallas as pl
from jax.experimental.pallas import tpu as pltpu

F32 = jnp.float32
BF16 = jnp.bfloat16

RMS_EPS = 1e-6

GDN_HEADS = 4
GDN_DK = 128
GDN_DV = 128
GDN_QK_W = GDN_HEADS * GDN_DK
GDN_V_W = GDN_HEADS * GDN_DV
GDN_QKV_W = 2 * GDN_QK_W + GDN_V_W
GDN_CONV = 4
GDN_CHUNK = 64
GDN_PAIR = 2 * GDN_CHUNK
GDN_TILE = 8 * GDN_PAIR
GDN_INV_BLOCK = 16

POOL_WINDOWS = (2, 4, 8, 16)
POOL_GROUP_W = 128
POOL_W = len(POOL_WINDOWS) * POOL_GROUP_W
POOL_HALO = 16

DIL_PATTERNS = ((128, 1), (512, 4), (2048, 16))
ATT_HEADS = 8
ATT_DH = 128
ATT_W = ATT_HEADS * ATT_DH
ATT_BLOCK = 128
ATT_QBLOCKS = 16
ATT_SLOPES = tuple(2.0 ** (-8.0 * (h + 1) / ATT_HEADS) for h in range(ATT_HEADS))
MASKED_SCORE = -1e30
LOG2_E = 1.4426950408889634
LN_2 = 0.6931471805599453

FFN_CONV = 3
FFN_HALO = 16
EVEN_HALO = 16
LANES = 128
ROW_TILE = 512
VMEM_LIMIT = 56 * 1024 * 1024


def _sigmoid(x):
    return 1.0 / (1.0 + jnp.exp(-x))


def _silu(x):
    return x * _sigmoid(x)


def _softplus(x):
    return jnp.maximum(x, 0.0) + jnp.log(1.0 + jnp.exp(-jnp.abs(x)))


def _mm(a, b):
    return jnp.dot(a, b, preferred_element_type=F32)


def _mm_nt(a, b):
    return lax.dot_general(a, b, (((1,), (1,)), ((), ())), preferred_element_type=F32)


def _mod_norm(x, gain, shift, scale):
    ms = jnp.mean(x * x, axis=-1, keepdims=True)
    return x * lax.rsqrt(ms + RMS_EPS) * (gain * (1.0 + scale)) + shift


def _params(semantics):
    return pltpu.CompilerParams(dimension_semantics=semantics, vmem_limit_bytes=VMEM_LIMIT)


def _resident(stack, layer, col_blocks=1, col=0):
    _, k, n = stack.shape
    return pl.BlockSpec((None, k, n // col_blocks), lambda *_: (layer, 0, col),
                        pipeline_mode=pl.Buffered(1))


def _ada_kernel(c_ref, w_ref, b_ref, o_ref):
    cs = _silu(c_ref[...])
    o_ref[...] = _mm(cs.astype(BF16), w_ref[...].astype(BF16)) + b_ref[...]


def _ada(c, ada_w, ada_b):
    depth, d, n = ada_w.shape
    b = c.shape[0]
    rows = 8
    cp = jnp.pad(c, ((0, rows - b), (0, 0)))
    tn = 1536
    out = pl.pallas_call(
        _ada_kernel,
        grid=(depth, n // tn),
        in_specs=[
            pl.BlockSpec((rows, d), lambda l, j: (0, 0)),
            pl.BlockSpec((None, d, tn), lambda l, j: (l, 0, j)),
            pl.BlockSpec((None, 1, tn), lambda l, j: (l, 0, j)),
        ],
        out_specs=pl.BlockSpec((None, rows, tn), lambda l, j: (l, 0, j)),
        out_shape=jax.ShapeDtypeStruct((depth, rows, n), F32),
        compiler_params=_params(("parallel", "parallel")),
        name="ada_mod",
    )(cp, ada_w, ada_b.reshape(depth, 1, n))
    return out[:, :b].reshape(depth, b, 6, d)


def _even_in_kernel(x_ref, xh_ref, mod_ref, gain_ref, w_ref, cw_ref, qkv_ref, z_ref, p_ref, ba_ref,
                    h_scr, g_scr):
    tm = x_ref.shape[0]
    halo = EVEN_HALO
    m = mod_ref[...]
    gain = gain_ref[...]
    shift, scale = m[0:1], m[1:2]
    h_scr[halo:, :] = _mod_norm(x_ref[...], gain, shift, scale).astype(BF16)
    live = jnp.where(pl.program_id(1) > 0, 1.0, 0.0)
    h_scr[0:halo, :] = (_mod_norm(xh_ref[...], gain, shift, scale) * live).astype(BF16)

    cw = cw_ref[...]
    width = g_scr.shape[-1]
    for lo in range(0, GDN_QKV_W, width):
        cs = slice(lo, lo + width)
        part = lo // GDN_QK_W
        g_scr[...] = _mm(h_scr[...], w_ref[:, cs])
        acc = cw[GDN_CONV - 1:GDN_CONV, cs] * g_scr[halo:, :]
        for j in range(GDN_CONV - 1):
            acc = acc + cw[j:j + 1, cs] * g_scr[pl.ds(halo - (GDN_CONV - 1) + j, tm), :]
        act = _silu(acc)
        for h in range(width // GDN_DK):
            a = act[:, h * GDN_DK:(h + 1) * GDN_DK]
            if part == 0:
                a = _l2norm(a) * (GDN_DK ** -0.5)
            elif part == 1:
                a = _l2norm(a)
            qkv_ref[:, lo + h * GDN_DK:lo + (h + 1) * GDN_DK] = a.astype(qkv_ref.dtype)

    col = GDN_QKV_W
    for ref in (z_ref, p_ref, ba_ref):
        n = ref.shape[-1]
        ref[...] = _mm(h_scr[halo:, :], w_ref[:, col:col + n]).astype(ref.dtype)
        col += n


def _even_in(x, mod, gain, w, layer, conv_w):
    b, t, d = x.shape
    tm = ROW_TILE
    per = tm // EVEN_HALO
    row = lambda width: pl.BlockSpec((None, tm, width), lambda bi, i: (bi, i, 0))
    return pl.pallas_call(
        _even_in_kernel,
        grid=(b, t // tm),
        in_specs=[
            row(d),
            pl.BlockSpec((None, EVEN_HALO, d), lambda bi, i: (bi, jnp.maximum(i * per - 1, 0), 0)),
            pl.BlockSpec((None, 6, d), lambda bi, i: (bi, 0, 0)),
            pl.BlockSpec((1, d), lambda bi, i: (0, 0)),
            _resident(w, layer),
            pl.BlockSpec(conv_w.shape, lambda bi, i: (0, 0)),
        ],
        out_specs=[row(GDN_QKV_W), row(GDN_V_W), row(POOL_W), row(LANES)],
        out_shape=[
            jax.ShapeDtypeStruct((b, t, GDN_QKV_W), BF16),
            jax.ShapeDtypeStruct((b, t, GDN_V_W), BF16),
            jax.ShapeDtypeStruct((b, t, POOL_W), BF16),
            jax.ShapeDtypeStruct((b, t, LANES), F32),
        ],
        scratch_shapes=[pltpu.VMEM((tm + EVEN_HALO, d), BF16),
                        pltpu.VMEM((tm + EVEN_HALO, 2 * GDN_DK), F32)],
        compiler_params=_params(("parallel", "parallel")),
        name="even_in_proj",
    )(x, x, mod, gain, w, conv_w)


def _lane_bcast(x, lane):
    return jnp.broadcast_to(x[:, lane:lane + 1], (x.shape[0], LANES))


def _row_bcast(x, row, rows):
    return jnp.broadcast_to(x[row:row + 1, :], (rows, x.shape[1]))


def _l2norm(x):
    return x * lax.rsqrt(jnp.sum(x * x, axis=-1, keepdims=True) + RMS_EPS)


def _unit_lower_solve(lows, rhss, blk, eye):
    bf = lambda xs: [x.astype(BF16) for x in xs]
    mm = lambda xs, ys: [_mm(x, y) for x, y in zip(xs, ys)]
    d = [jnp.where(blk, low, 0.0) for low in lows]
    n = bf([low - di for low, di in zip(lows, d)])
    db = bf(d)
    d2 = mm(db, db)
    d2b = bf(d2)
    d4 = mm(d2b, d2b)
    d4b = bf(d4)
    d8 = mm(d4b, d4b)
    acc = mm(bf([eye - x for x in d]), bf([eye + x for x in d2]))
    acc = mm(bf(acc), bf([eye + x for x in d4]))
    dinv = bf(mm(bf(acc), bf([eye + x for x in d8])))
    m = mm(dinv, n)
    mb = bf(m)
    m2 = mm(mb, mb)
    t1 = bf(mm(dinv, rhss))
    t2 = bf(mm(bf([eye + x for x in m2]), t1))
    return mm(bf([eye - x for x in m]), t2)


def _gdn_pool_kernel(qkv_ref, z_ref, p_ref, ba_ref, alog_ref, dtb_ref, gnorm_ref,
                     poolw_ref, pscale_ref, wout_ref, x_ref, mod_ref, o_ref, p_ext, s_scr, mix_scr):
    tt = GDN_TILE
    ch = GDN_CHUNK
    t = pl.program_id(1)

    @pl.when(t == 0)
    def _():
        p_ext[0:POOL_HALO, :] = jnp.zeros((POOL_HALO, POOL_W), F32)
        s_scr[...] = jnp.zeros(s_scr.shape, F32)

    p_ext[POOL_HALO:POOL_HALO + tt, :] = p_ref[...].astype(F32)

    pt = GDN_PAIR
    row = lax.broadcasted_iota(jnp.int32, (pt, pt), 0)
    col = lax.broadcasted_iota(jnp.int32, (pt, pt), 1)
    chunk_of = lambda idx: jnp.right_shift(idx, ch.bit_length() - 1)
    block_of = lambda idx: jnp.right_shift(idx, GDN_INV_BLOCK.bit_length() - 1)
    same = chunk_of(row) == chunk_of(col)
    causal = jnp.logical_and(same, row >= col)
    strict = jnp.logical_and(same, row > col)
    blk = block_of(row) == block_of(col)
    eye = jnp.where(row == col, 1.0, 0.0).astype(F32)
    first = row < ch
    tri = jnp.where(causal, 1.0, 0.0).astype(F32)

    pairs = tt // pt
    units = [(p, h) for p in range(pairs) for h in range(GDN_HEADS)]
    rows = lambda p: slice(p * pt, (p + 1) * pt)
    per_unit = lambda fn: [fn(p, h) for p, h in units]
    bf = lambda xs: [x.astype(BF16) for x in xs]

    beta_all, gc_all, gc_all_t = [], [], []
    for p in range(pairs):
        ba = ba_ref[rows(p), :]
        beta_all.append(_sigmoid(ba))
        g = -jnp.exp(alog_ref[...]) * _softplus(ba + dtb_ref[...])
        gc = jnp.dot(tri, g, preferred_element_type=F32, precision=lax.Precision.HIGHEST)
        gc_all.append(gc)
        gc_all_t.append(gc.T)

    q16 = per_unit(lambda p, h: qkv_ref[rows(p), h * GDN_DK:(h + 1) * GDN_DK])
    k16 = per_unit(lambda p, h: qkv_ref[rows(p), GDN_QK_W + h * GDN_DK:GDN_QK_W + (h + 1) * GDN_DK])
    v16 = per_unit(lambda p, h: qkv_ref[rows(p),
                                        2 * GDN_QK_W + h * GDN_DV:2 * GDN_QK_W + (h + 1) * GDN_DV])
    q = [x.astype(F32) for x in q16]
    k = [x.astype(F32) for x in k16]
    v = [x.astype(F32) for x in v16]
    beta = per_unit(lambda p, h: _lane_bcast(beta_all[p], h))
    gcol = per_unit(lambda p, h: _lane_bcast(gc_all[p], GDN_HEADS + h))
    grow = per_unit(lambda p, h: _row_bcast(gc_all_t[p], GDN_HEADS + h, pt))
    decay = [jnp.where(causal, jnp.exp(jnp.where(causal, gi - gj, 0.0)), 0.0)
             for gi, gj in zip(gcol, grow)]
    egc = [jnp.exp(g) for g in gcol]
    kb = [ki * bi for ki, bi in zip(k, beta)]
    low = [jnp.where(strict, _mm_nt(a, b) * d, 0.0) for a, b, d in zip(bf(kb), k16, decay)]
    intra = bf([_mm_nt(a, b) * d for a, b, d in zip(q16, k16, decay)])
    rhs = bf([jnp.concatenate([vi * bi, kbi * ei], axis=1)
              for vi, bi, kbi, ei in zip(v, beta, kb, egc)])
    uw = bf(_unit_lower_solve(low, rhs, blk, eye))

    iu = [_mm(a, b) for a, b in zip(intra, uw)]
    o_const = [x[:, :GDN_DV] for x in iu]
    q_eff = bf([qi * ei - x[:, GDN_DV:] for qi, ei, x in zip(q, egc, iu)])
    glast = [jnp.where(first, _row_bcast(g, ch - 1, pt), _row_bcast(g, pt - 1, pt)) for g in gcol]
    kdec = [ki * jnp.exp(gl - g) for ki, gl, g in zip(k, glast, gcol)]
    chunks = [slice(c * ch, (c + 1) * ch) for c in range(pt // ch)]
    ku = [[_mm(kd[rs].T.astype(BF16), x[rs]) for rs in chunks] for kd, x in zip(kdec, uw)]

    s = [s_scr[h] for h in range(GDN_HEADS)]
    outs = {}
    for p in range(pairs):
        for c, rs in enumerate(chunks):
            for h in range(GDN_HEADS):
                i = p * GDN_HEADS + h
                sb = s[h].astype(BF16)
                outs[p, h, c] = _mm(q_eff[i][rs], sb) + o_const[i][rs]
                gl = _row_bcast(egc[i], rs.stop - 1, GDN_DK)
                s[h] = gl * s[h] + ku[i][c][:, :GDN_DV] - _mm(ku[i][c][:, GDN_DV:].astype(BF16), sb)
    for h in range(GDN_HEADS):
        s_scr[h] = s[h]

    gnorm = gnorm_ref[...]
    for p, h in units:
        hs = slice(h * GDN_DV, (h + 1) * GDN_DV)
        o = jnp.concatenate([outs[p, h, c] for c in range(len(chunks))], axis=0)
        o = o * lax.rsqrt(jnp.mean(o * o, axis=-1, keepdims=True) + RMS_EPS) * gnorm
        mix_scr[rows(p), hs] = (o * _silu(z_ref[rows(p), hs].astype(F32))).astype(mix_scr.dtype)

    tok = (t * tt + 1 + lax.broadcasted_iota(jnp.int32, (tt, POOL_GROUP_W), 0)).astype(F32)
    pscale = pscale_ref[...]
    for gi, win in enumerate(POOL_WINDOWS):
        cs = slice(gi * POOL_GROUP_W, (gi + 1) * POOL_GROUP_W)
        cur = p_ext[POOL_HALO:POOL_HALO + tt, cs]
        tot = cur
        for back in range(1, win):
            tot = tot + p_ext[pl.ds(POOL_HALO - back, tt), cs]
        pooled = tot / jnp.minimum(tok, float(win)) - cur
        y = _mm(pooled.astype(BF16), poolw_ref[gi])
        mix_scr[:, GDN_V_W + gi * POOL_GROUP_W:GDN_V_W + (gi + 1) * POOL_GROUP_W] = (
            y * pscale[:, cs]).astype(mix_scr.dtype)

    p_ext[0:POOL_HALO, :] = p_ext[tt:tt + POOL_HALO, :]

    gate = mod_ref[...][2:3]
    o_ref[...] = x_ref[...] + gate * _mm(mix_scr[...], wout_ref[...])


def _gdn_pool(qkv, z, p, ba, alog_row, dtb_row, gnorm, pool_w, pool_scale, w_out, layer, x, mod):
    b, t, d = x.shape
    tt = GDN_TILE
    row = lambda width: pl.BlockSpec((None, tt, width), lambda bi, i: (bi, i, 0))
    full = lambda a: pl.BlockSpec(a.shape, lambda bi, i: (0,) * a.ndim)
    small = (alog_row, dtb_row, gnorm, pool_w, pool_scale)
    return pl.pallas_call(
        _gdn_pool_kernel,
        grid=(b, t // tt),
        in_specs=[row(GDN_QKV_W), row(GDN_V_W), row(POOL_W), row(LANES)] + [full(a) for a in small] + [
            _resident(w_out, layer), row(d), pl.BlockSpec((None, 6, d), lambda bi, i: (bi, 0, 0))],
        out_specs=row(d),
        out_shape=jax.ShapeDtypeStruct((b, t, d), F32),
        scratch_shapes=[
            pltpu.VMEM((tt + POOL_HALO, POOL_W), F32),
            pltpu.VMEM((GDN_HEADS, GDN_DK, GDN_DV), F32),
            pltpu.VMEM((tt, GDN_V_W + POOL_W), BF16),
        ],
        compiler_params=_params(("parallel", "arbitrary")),
        name="gdn_pool_out_proj",
    )(qkv, z, p, ba, *small, w_out, x, mod)


def _out_res_kernel(a_ref, w_ref, x_ref, mod_ref, o_ref, *, gate_row):
    gate = mod_ref[...][gate_row:gate_row + 1]
    o_ref[...] = x_ref[...] + gate * _mm(a_ref[...], w_ref[...])


def _out_res(a, w, layer, x, mod, gate_row):
    b, t, d = x.shape
    k = a.shape[-1]
    tm = 2 * ROW_TILE
    return pl.pallas_call(
        functools.partial(_out_res_kernel, gate_row=gate_row),
        grid=(b, t // tm),
        in_specs=[
            pl.BlockSpec((None, tm, k), lambda bi, i: (bi, i, 0)),
            _resident(w, layer),
            pl.BlockSpec((None, tm, d), lambda bi, i: (bi, i, 0)),
            pl.BlockSpec((None, 6, d), lambda bi, i: (bi, 0, 0)),
        ],
        out_specs=pl.BlockSpec((None, tm, d), lambda bi, i: (bi, i, 0)),
        out_shape=jax.ShapeDtypeStruct((b, t, d), F32),
        compiler_params=_params(("parallel", "parallel")),
        name="out_proj_residual",
    )(a, w, x, mod)


def _odd_in_kernel(x_ref, mod_ref, gain_ref, w_ref, qn_ref, kn_ref, o0_ref, o1_ref, o2_ref,
                   hn_slab, h_scr):
    tm = x_ref.shape[0]
    d = x_ref.shape[1]
    m = mod_ref[...]
    hn = _mod_norm(x_ref[...], gain_ref[...], m[0:1], m[1:2])
    for gi, (_, dil) in enumerate(DIL_PATTERNS):
        if dil == 1:
            h_scr[gi] = hn.astype(BF16)
    for sb in range(d // LANES):
        hn_slab[sb] = hn[:, sb * LANES:(sb + 1) * LANES]
    for gi, (_, dil) in enumerate(DIL_PATTERNS):
        per = tm // dil
        for r in range(dil if dil > 1 else 0):
            for sb in range(d // LANES):
                h_scr[gi, r * per:(r + 1) * per, sb * LANES:(sb + 1) * LANES] = (
                    hn_slab[sb, pl.ds(r, per, stride=dil), :].astype(BF16))
    qn = qn_ref[...] * (ATT_DH ** -0.5 * LOG2_E)
    kn = kn_ref[...]
    half = ATT_W // 2
    for gi, o_ref in enumerate((o0_ref, o1_ref, o2_ref)):
        dil = DIL_PATTERNS[gi][1]
        per = tm // dil
        for blk in range(3 * ATT_W // half):
            lo = blk * half
            y = _mm(h_scr[gi], w_ref[:, gi * 3 * ATT_W + lo:gi * 3 * ATT_W + lo + half])
            part = lo // ATT_W
            for h in range(half // ATT_DH):
                cs = slice(lo + h * ATT_DH, lo + (h + 1) * ATT_DH)
                yh = y[:, h * ATT_DH:(h + 1) * ATT_DH]
                if part < 2:
                    ms = jnp.mean(yh * yh, axis=-1, keepdims=True)
                    yh = yh * lax.rsqrt(ms + RMS_EPS) * (qn if part == 0 else kn)
                yh = yh.astype(o_ref.dtype)
                for r in range(dil):
                    o_ref[r, :, cs] = yh[r * per:(r + 1) * per]


def _odd_in(x, mod, gain, w, layer, q_norm, k_norm):
    b, t, d = x.shape
    tm = ROW_TILE
    dils = [dl for _, dl in DIL_PATTERNS]
    return pl.pallas_call(
        _odd_in_kernel,
        grid=(b, t // tm),
        in_specs=[
            pl.BlockSpec((None, tm, d), lambda bi, i: (bi, i, 0)),
            pl.BlockSpec((None, 6, d), lambda bi, i: (bi, 0, 0)),
            pl.BlockSpec((1, d), lambda bi, i: (0, 0)),
            _resident(w, layer),
            pl.BlockSpec((1, ATT_DH), lambda bi, i: (0, 0)),
            pl.BlockSpec((1, ATT_DH), lambda bi, i: (0, 0)),
        ],
        out_specs=[pl.BlockSpec((None, dl, tm // dl, 3 * ATT_W), lambda bi, i: (bi, 0, i, 0))
                   for dl in dils],
        out_shape=[jax.ShapeDtypeStruct((b, dl, t // dl, 3 * ATT_W), BF16) for dl in dils],
        scratch_shapes=[pltpu.VMEM((d // LANES, tm, LANES), F32),
                        pltpu.VMEM((len(dils), tm, d), BF16)],
        compiler_params=_params(("parallel", "parallel")),
        name="odd_in_proj",
    )(x, mod, gain, w, q_norm, k_norm)


def _attn_kernel(q_ref, kp_ref, kc_ref, vp_ref, vc_ref, o_ref, stat_ref, bias_scr, *, dil, n_back):
    blk = ATT_BLOCK
    first_step = jnp.logical_and(jnp.logical_and(pl.program_id(0) == 0, pl.program_id(1) == 0),
                                 pl.program_id(2) == 0)

    @pl.when(first_step)
    def _():
        a = lax.broadcasted_iota(jnp.int32, (blk, 2 * blk), 0)
        j = lax.broadcasted_iota(jnp.int32, (blk, 2 * blk), 1)
        rel = blk + a - j
        in_band = jnp.logical_and(rel >= 0, rel <= n_back)
        own = jnp.logical_and(in_band, j >= blk)
        relf = rel.astype(F32)
        for h in range(ATT_HEADS):
            bias = -(ATT_SLOPES[h] * dil * LOG2_E) * relf
            bias_scr[0, h] = jnp.where(own, bias, MASKED_SCORE)
            bias_scr[1, h] = jnp.where(in_band, bias, MASKED_SCORE)

    heads = [slice(h * ATT_DH, (h + 1) * ATT_DH) for h in range(ATT_HEADS)]
    lane = lax.broadcasted_iota(jnp.int32, (blk, LANES), 1)
    n_res, n_rows = q_ref.shape[0], q_ref.shape[1]
    for c, i in [(c, i) for c in range(n_res) for i in range(n_rows // blk)]:
        rows = slice(i * blk, (i + 1) * blk)
        if i == 0:
            variant = jnp.minimum(pl.program_id(2), 1)
            keys = lambda hs, c=c, rows=rows: jnp.concatenate([kp_ref[c, :, hs], kc_ref[c, rows, hs]],
                                                              axis=0)
            vals = lambda hs, c=c, rows=rows: jnp.concatenate([vp_ref[c, :, hs], vc_ref[c, rows, hs]],
                                                              axis=0)
        else:
            variant = 1
            both = slice((i - 1) * blk, (i + 1) * blk)
            keys = lambda hs, c=c, both=both: kc_ref[c, both, hs]
            vals = lambda hs, c=c, both=both: vc_ref[c, both, hs]
        s = [_mm_nt(q_ref[c, rows, hs], keys(hs)) + bias_scr[variant, h]
             for h, hs in enumerate(heads)]
        m = [jnp.max(sh, axis=-1, keepdims=True) for sh in s]
        p = [jnp.exp2(sh - mh) for sh, mh in zip(s, m)]
        l = [jnp.sum(ph, axis=-1, keepdims=True) for ph in p]
        o = [_mm(ph.astype(BF16), vals(hs)) for ph, hs in zip(p, heads)]
        stats = jnp.zeros((blk, LANES), F32)
        for h, hs in enumerate(heads):
            o_ref[c, rows, hs] = o[h].astype(o_ref.dtype)
            stats = jnp.where(lane == h, m[h], stats)
            stats = jnp.where(lane == ATT_HEADS + h, l[h], stats)
        stat_ref[c, rows, :] = stats


def _attention_group(qkv, window, dil):
    b, _, length, _ = qkv.shape
    qblocks = min(ATT_QBLOCKS, length // ATT_BLOCK)
    rows = qblocks * ATT_BLOCK
    n_res = min(dil, ATT_QBLOCKS // qblocks)
    assert length % rows == 0 and dil % n_res == 0

    def cur(part):
        return pl.BlockSpec((None, n_res, rows, ATT_W), lambda bi, r, n: (bi, r, n, part))

    def prev(part):
        return pl.BlockSpec((None, n_res, ATT_BLOCK, ATT_W),
                            lambda bi, r, n: (bi, r, jnp.maximum(n * qblocks - 1, 0), part))

    return pl.pallas_call(
        functools.partial(_attn_kernel, dil=dil, n_back=window // dil),
        grid=(b, dil // n_res, length // rows),
        in_specs=[cur(0), prev(1), cur(1), prev(2), cur(2)],
        out_specs=[
            pl.BlockSpec((None, n_res, rows, ATT_W), lambda bi, r, n: (bi, r, n, 0)),
            pl.BlockSpec((None, n_res, rows, LANES), lambda bi, r, n: (bi, r, n, 0)),
        ],
        out_shape=[
            jax.ShapeDtypeStruct((b, dil, length, ATT_W), BF16),
            jax.ShapeDtypeStruct((b, dil, length, LANES), F32),
        ],
        scratch_shapes=[pltpu.VMEM((2, ATT_HEADS, ATT_BLOCK, 2 * ATT_BLOCK), F32)],
        compiler_params=_params(("arbitrary", "arbitrary", "arbitrary")),
        name=f"dilated_attention_d{dil}",
    )(qkv, qkv, qkv, qkv, qkv)


def _merge_out_kernel(o0_ref, o1_ref, o2_ref, l0_ref, l1_ref, l2_ref, w_ref, x_ref, mod_ref,
                      out_ref, o_slab, l_slab, a_scr):
    tm = x_ref.shape[0]
    o_refs = (o0_ref, o1_ref, o2_ref)
    l_refs = (l0_ref, l1_ref, l2_ref)
    heads = [slice(h * ATT_DH, (h + 1) * ATT_DH) for h in range(ATT_HEADS)]
    for gi, (_, dil) in enumerate(DIL_PATTERNS):
        if dil == 1:
            continue
        for r in range(dil):
            dst = pl.ds(r, tm // dil, stride=dil)
            l_slab[gi, dst, :] = l_refs[gi][r]
            for h, hs in enumerate(heads):
                o_slab[gi, h, dst, :] = o_refs[gi][r, :, hs].astype(F32)

    def lse_of(gi):
        return l_refs[gi][0] if DIL_PATTERNS[gi][1] == 1 else l_slab[gi]

    def out_of(gi, h):
        if DIL_PATTERNS[gi][1] == 1:
            return o_refs[gi][0, :, heads[h]].astype(F32)
        return o_slab[gi, h]

    stats = [lse_of(gi) for gi in range(len(DIL_PATTERNS))]
    dens = [pltpu.roll(st, LANES - ATT_HEADS, axis=1) for st in stats]
    mx = jnp.maximum(jnp.maximum(stats[0], stats[1]), stats[2])
    es = [jnp.exp2(st - mx) for st in stats]
    total = es[0] * dens[0] + es[1] * dens[1] + es[2] * dens[2]
    head_lane = lax.broadcasted_iota(jnp.int32, total.shape, 1) < ATT_HEADS
    inv = 1.0 / jnp.where(head_lane, total, 1.0)
    wts = [e * inv for e in es]
    for h, hs in enumerate(heads):
        merged = (_lane_bcast(wts[0], h) * out_of(0, h) + _lane_bcast(wts[1], h) * out_of(1, h)
                  + _lane_bcast(wts[2], h) * out_of(2, h))
        a_scr[:, hs] = merged.astype(BF16)
    gate = mod_ref[...][2:3]
    out_ref[...] = x_ref[...] + gate * _mm(a_scr[...], w_ref[...])


def _merge_out(outs, lses, w, layer, x, mod):
    b, t, d = x.shape
    tm = 2 * ROW_TILE
    groups = len(DIL_PATTERNS)
    row = lambda width: pl.BlockSpec((None, tm, width), lambda bi, i: (bi, i, 0))
    by_residue = lambda dl, width: pl.BlockSpec((None, dl, tm // dl, width),
                                                lambda bi, i: (bi, 0, i, 0))
    dils = [dl for _, dl in DIL_PATTERNS]
    return pl.pallas_call(
        _merge_out_kernel,
        grid=(b, t // tm),
        in_specs=[by_residue(dl, ATT_W) for dl in dils] + [by_residue(dl, LANES) for dl in dils] + [
            _resident(w, layer), row(d), pl.BlockSpec((None, 6, d), lambda bi, i: (bi, 0, 0))],
        out_specs=row(d),
        out_shape=jax.ShapeDtypeStruct((b, t, d), F32),
        scratch_shapes=[
            pltpu.VMEM((groups, ATT_HEADS, tm, ATT_DH), F32),
            pltpu.VMEM((groups, tm, LANES), F32),
            pltpu.VMEM((tm, ATT_W), BF16),
        ],
        compiler_params=_params(("parallel", "parallel")),
        name="attn_merge_out_proj",
    )(*outs, *lses, w, x, mod)


def _ffn_up_kernel(x_ref, xh_ref, mod_ref, gain_ref, wg_ref, wv_ref, cw_ref, cb_ref, o_ref,
                   h_scr, g_scr, a_scr, *, chunk):
    tm = x_ref.shape[0]
    halo = FFN_HALO
    m = mod_ref[...]
    gain = gain_ref[...]
    shift, scale = m[3:4], m[4:5]
    h_scr[halo:, :] = _mod_norm(x_ref[...], gain, shift, scale).astype(BF16)
    live = jnp.where(pl.program_id(1) > 0, 1.0, 0.0)
    h_scr[0:halo, :] = (_mod_norm(xh_ref[...], gain, shift, scale) * live).astype(BF16)
    cw = cw_ref[...]
    cb = cb_ref[...]
    n_chunks = o_ref.shape[-1] // chunk
    for c in range(n_chunks):
        cs = slice(c * chunk, (c + 1) * chunk)
        g_scr[...] = _mm(h_scr[...], wg_ref[:, cs])
        gate = cb[:, cs] + cw[FFN_CONV - 1:FFN_CONV, cs] * g_scr[halo:, :]
        for j in range(FFN_CONV - 1):
            gate = gate + cw[j:j + 1, cs] * g_scr[pl.ds(halo - (FFN_CONV - 1) + j, tm), :]
        a_scr[:, cs] = _silu(gate)
    for c in range(n_chunks):
        cs = slice(c * chunk, (c + 1) * chunk)
        o_ref[:, cs] = (a_scr[:, cs] * _mm(h_scr[halo:, :], wv_ref[:, cs])).astype(o_ref.dtype)


def _ffn_up(x, mod, gain, w_up, layer, conv_w, conv_b):
    b, t, d = x.shape
    f = w_up.shape[-1] // 2
    tm = ROW_TILE
    chunk = 256
    per = tm // FFN_HALO
    half = lambda j: _resident(w_up, layer, col_blocks=2, col=j)
    return pl.pallas_call(
        functools.partial(_ffn_up_kernel, chunk=chunk),
        grid=(b, t // tm),
        in_specs=[
            pl.BlockSpec((None, tm, d), lambda bi, i: (bi, i, 0)),
            pl.BlockSpec((None, FFN_HALO, d), lambda bi, i: (bi, jnp.maximum(i * per - 1, 0), 0)),
            pl.BlockSpec((None, 6, d), lambda bi, i: (bi, 0, 0)),
            pl.BlockSpec((1, d), lambda bi, i: (0, 0)),
            half(0),
            half(1),
            pl.BlockSpec(conv_w.shape, lambda bi, i: (0, 0)),
            pl.BlockSpec(conv_b.shape, lambda bi, i: (0, 0)),
        ],
        out_specs=pl.BlockSpec((None, tm, f), lambda bi, i: (bi, i, 0)),
        out_shape=jax.ShapeDtypeStruct((b, t, f), BF16),
        scratch_shapes=[pltpu.VMEM((tm + FFN_HALO, d), BF16), pltpu.VMEM((tm + FFN_HALO, chunk), F32),
                        pltpu.VMEM((tm, f), F32)],
        compiler_params=_params(("parallel", "parallel")),
        name="ffn_up_conv_gate",
    )(x, x, mod, gain, w_up, w_up, conv_w, conv_b)


def _lane_row(values, offset):
    return jnp.zeros((1, LANES), F32).at[0, offset:offset + values.shape[0]].set(values)


def kernel(x, c, ada_w, ada_b, norm_mix, norm_ffn, ev_w_in, ev_w_out, gdn_conv_w, gdn_a_log,
           gdn_dt_bias, gdn_norm, pool_w, pool_scale, od_w_in, od_w_out, att_q_norm, att_k_norm,
           ffn_w_up, ffn_conv_w, ffn_conv_b, ffn_w_down):
    b, t, d = x.shape
    depth = ada_w.shape[0]
    assert t % (ATT_BLOCK * max(dl for _, dl in DIL_PATTERNS)) == 0 and t % (2 * ROW_TILE) == 0
    d_ff = ffn_w_down.shape[1]

    mod = _ada(c, ada_w, ada_b)

    i1 = GDN_QKV_W + GDN_V_W
    i2 = i1 + 2 * GDN_HEADS
    pad = jnp.zeros(ev_w_in.shape[:2] + (LANES - 2 * GDN_HEADS,), ev_w_in.dtype)
    ev_w = jnp.concatenate([ev_w_in[..., :i1], ev_w_in[..., i2:], ev_w_in[..., i1:i2], pad],
                           axis=-1).astype(BF16)
    ev_wo, od_w, od_wo = ev_w_out.astype(BF16), od_w_in.astype(BF16), od_w_out.astype(BF16)
    up_w, down_w = ffn_w_up.astype(BF16), ffn_w_down.astype(BF16)

    for i in range(depth):
        m = mod[i]
        gain_m = norm_mix[i].reshape(1, d)
        if i % 2 == 0:
            e = i // 2
            qkv, z, p, ba = _even_in(x, m, gain_m, ev_w, e, gdn_conv_w[e])
            x = _gdn_pool(
                qkv, z, p, ba,
                _lane_row(gdn_a_log[e], GDN_HEADS), _lane_row(gdn_dt_bias[e], GDN_HEADS),
                gdn_norm[e].reshape(1, GDN_DV), pool_w[e].astype(BF16),
                pool_scale[e].reshape(1, POOL_W), ev_wo, e, x, m)
        else:
            o = i // 2
            qkvs = _odd_in(x, m, gain_m, od_w, o,
                           att_q_norm[o].reshape(1, ATT_DH), att_k_norm[o].reshape(1, ATT_DH))
            outs, lses = [], []
            for qkv, (window, dil) in zip(qkvs, DIL_PATTERNS):
                og, lg = _attention_group(qkv, window, dil)
                outs.append(og)
                lses.append(lg)
            x = _merge_out(outs, lses, od_wo, o, x, m)

        act = _ffn_up(x, m, norm_ffn[i].reshape(1, d), up_w, i,
                      ffn_conv_w[i], ffn_conv_b[i].reshape(1, d_ff))
        x = _out_res(act, down_w, i, x, m, 5)
    return x
```

```python
import functools

import jax
import jax.numpy as jnp
from jax import lax
from jax.experimental import pallas as pl
from jax.experimental.pallas import tpu as pltpu

F32 = jnp.float32
BF16 = jnp.bfloat16

RMS_EPS = 1e-6

GDN_HEADS = 4
GDN_DK = 128
GDN_DV = 128
GDN_QK_W = GDN_HEADS * GDN_DK
GDN_V_W = GDN_HEADS * GDN_DV
GDN_QKV_W = 2 * GDN_QK_W + GDN_V_W
GDN_CONV = 4
GDN_CHUNK = 64
GDN_PAIR = 2 * GDN_CHUNK
GDN_TILE = 4 * GDN_PAIR
GDN_INV_BLOCK = 16

POOL_WINDOWS = (2, 4, 8, 16)
POOL_GROUP_W = 128
POOL_W = len(POOL_WINDOWS) * POOL_GROUP_W
POOL_HALO = 16

DIL_PATTERNS = ((128, 1), (512, 4), (2048, 16))
ATT_HEADS = 8
ATT_DH = 128
ATT_W = ATT_HEADS * ATT_DH
ATT_BLOCK = 128
ATT_QBLOCKS = 8
ATT_SLOPES = tuple(2.0 ** (-8.0 * (h + 1) / ATT_HEADS) for h in range(ATT_HEADS))
MASKED_SCORE = -1e30
LOG2_E = 1.4426950408889634
LN_2 = 0.6931471805599453

FFN_CONV = 3
FFN_HALO = 16
EVEN_HALO = 16
LANES = 128
ROW_TILE = 512
VMEM_LIMIT = 56 * 1024 * 1024


def _sigmoid(x):
    return 1.0 / (1.0 + jnp.exp(-x))


def _silu(x):
    return x * _sigmoid(x)


def _softplus(x):
    return jnp.maximum(x, 0.0) + jnp.log(1.0 + jnp.exp(-jnp.abs(x)))


def _mm(a, b):
    return jnp.dot(a, b, preferred_element_type=F32)


def _mm_nt(a, b):
    return lax.dot_general(a, b, (((1,), (1,)), ((), ())), preferred_element_type=F32)


def _mod_norm(x, gain, shift, scale):
    ms = jnp.mean(x * x, axis=-1, keepdims=True)
    return x * lax.rsqrt(ms + RMS_EPS) * (gain * (1.0 + scale)) + shift


def _params(semantics):
    return pltpu.CompilerParams(dimension_semantics=semantics, vmem_limit_bytes=VMEM_LIMIT)


def _resident(stack, layer, col_blocks=1, col=0):
    _, k, n = stack.shape
    return pl.BlockSpec((None, k, n // col_blocks), lambda *_: (layer, 0, col),
                        pipeline_mode=pl.Buffered(1))


def _ada_kernel(c_ref, w_ref, b_ref, o_ref):
    cs = _silu(c_ref[...])
    o_ref[...] = _mm(cs.astype(BF16), w_ref[...].astype(BF16)) + b_ref[...]


def _ada(c, ada_w, ada_b):
    depth, d, n = ada_w.shape
    b = c.shape[0]
    rows = 8
    cp = jnp.pad(c, ((0, rows - b), (0, 0)))
    tn = 1536
    out = pl.pallas_call(
        _ada_kernel,
        grid=(depth, n // tn),
        in_specs=[
            pl.BlockSpec((rows, d), lambda l, j: (0, 0)),
            pl.BlockSpec((None, d, tn), lambda l, j: (l, 0, j)),
            pl.BlockSpec((None, 1, tn), lambda l, j: (l, 0, j)),
        ],
        out_specs=pl.BlockSpec((None, rows, tn), lambda l, j: (l, 0, j)),
        out_shape=jax.ShapeDtypeStruct((depth, rows, n), F32),
        compiler_params=_params(("parallel", "parallel")),
        name="ada_mod",
    )(cp, ada_w, ada_b.reshape(depth, 1, n))
    return out[:, :b].reshape(depth, b, 6, d)


def _even_in_kernel(x_ref, xh_ref, mod_ref, gain_ref, w_ref, cw_ref, qkv_ref, z_ref, p_ref, ba_ref,
                    h_scr, g_scr):
    tm = x_ref.shape[0]
    halo = EVEN_HALO
    m = mod_ref[...]
    gain = gain_ref[...]
    shift, scale = m[0:1], m[1:2]
    h_scr[halo:, :] = _mod_norm(x_ref[...], gain, shift, scale).astype(BF16)
    live = jnp.where(pl.program_id(1) > 0, 1.0, 0.0)
    h_scr[0:halo, :] = (_mod_norm(xh_ref[...], gain, shift, scale) * live).astype(BF16)

    cw = cw_ref[...]
    width = g_scr.shape[-1]
    for lo in range(0, GDN_QKV_W, width):
        cs = slice(lo, lo + width)
        part = lo // GDN_QK_W
        g_scr[...] = _mm(h_scr[...], w_ref[:, cs])
        acc = cw[GDN_CONV - 1:GDN_CONV, cs] * g_scr[halo:, :]
        for j in range(GDN_CONV - 1):
            acc = acc + cw[j:j + 1, cs] * g_scr[pl.ds(halo - (GDN_CONV - 1) + j, tm), :]
        act = _silu(acc)
        for h in range(width // GDN_DK):
            a = act[:, h * GDN_DK:(h + 1) * GDN_DK]
            if part == 0:
                a = _l2norm(a) * (GDN_DK ** -0.5)
            elif part == 1:
                a = _l2norm(a)
            qkv_ref[:, lo + h * GDN_DK:lo + (h + 1) * GDN_DK] = a.astype(qkv_ref.dtype)

    col = GDN_QKV_W
    for ref in (z_ref, p_ref, ba_ref):
        n = ref.shape[-1]
        ref[...] = _mm(h_scr[halo:, :], w_ref[:, col:col + n]).astype(ref.dtype)
        col += n


def _even_in(x, mod, gain, w, layer, conv_w):
    b, t, d = x.shape
    tm = ROW_TILE
    per = tm // EVEN_HALO
    row = lambda width: pl.BlockSpec((None, tm, width), lambda bi, i: (bi, i, 0))
    return pl.pallas_call(
        _even_in_kernel,
        grid=(b, t // tm),
        in_specs=[
            row(d),
            pl.BlockSpec((None, EVEN_HALO, d), lambda bi, i: (bi, jnp.maximum(i * per - 1, 0), 0)),
            pl.BlockSpec((None, 6, d), lambda bi, i: (bi, 0, 0)),
            pl.BlockSpec((1, d), lambda bi, i: (0, 0)),
            _resident(w, layer),
            pl.BlockSpec(conv_w.shape, lambda bi, i: (0, 0)),
        ],
        out_specs=[row(GDN_QKV_W), row(GDN_V_W), row(POOL_W), row(LANES)],
        out_shape=[
            jax.ShapeDtypeStruct((b, t, GDN_QKV_W), BF16),
            jax.ShapeDtypeStruct((b, t, GDN_V_W), BF16),
            jax.ShapeDtypeStruct((b, t, POOL_W), BF16),
            jax.ShapeDtypeStruct((b, t, LANES), F32),
        ],
        scratch_shapes=[pltpu.VMEM((tm + EVEN_HALO, d), BF16),
                        pltpu.VMEM((tm + EVEN_HALO, 2 * GDN_DK), F32)],
        compiler_params=_params(("parallel", "parallel")),
        name="even_in_proj",
    )(x, x, mod, gain, w, conv_w)


def _lane_bcast(x, lane):
    return jnp.broadcast_to(x[:, lane:lane + 1], (x.shape[0], LANES))


def _row_bcast(x, row, rows):
    return jnp.broadcast_to(x[row:row + 1, :], (rows, x.shape[1]))


def _l2norm(x):
    return x * lax.rsqrt(jnp.sum(x * x, axis=-1, keepdims=True) + RMS_EPS)


def _unit_lower_solve(lows, rhss, blk, eye):
    bf = lambda xs: [x.astype(BF16) for x in xs]
    mm = lambda xs, ys: [_mm(x, y) for x, y in zip(xs, ys)]
    d = [jnp.where(blk, low, 0.0) for low in lows]
    n = bf([low - di for low, di in zip(lows, d)])
    db = bf(d)
    d2 = mm(db, db)
    d2b = bf(d2)
    d4 = mm(d2b, d2b)
    d4b = bf(d4)
    d8 = mm(d4b, d4b)
    acc = mm(bf([eye - x for x in d]), bf([eye + x for x in d2]))
    acc = mm(bf(acc), bf([eye + x for x in d4]))
    dinv = bf(mm(bf(acc), bf([eye + x for x in d8])))
    m = mm(dinv, n)
    mb = bf(m)
    m2 = mm(mb, mb)
    t1 = bf(mm(dinv, rhss))
    t2 = bf(mm(bf([eye + x for x in m2]), t1))
    return mm(bf([eye - x for x in m]), t2)


def _gdn_pool_kernel(qkv_ref, z_ref, p_ref, ba_ref, alog_ref, dtb_ref, gnorm_ref,
                     poolw_ref, pscale_ref, wout_ref, x_ref, mod_ref, o_ref, p_ext, s_scr, mix_scr):
    tt = GDN_TILE
    ch = GDN_CHUNK
    t = pl.program_id(1)

    @pl.when(t == 0)
    def _():
        p_ext[0:POOL_HALO, :] = jnp.zeros((POOL_HALO, POOL_W), F32)
        s_scr[...] = jnp.zeros(s_scr.shape, F32)

    p_ext[POOL_HALO:POOL_HALO + tt, :] = p_ref[...].astype(F32)

    pt = GDN_PAIR
    row = lax.broadcasted_iota(jnp.int32, (pt, pt), 0)
    col = lax.broadcasted_iota(jnp.int32, (pt, pt), 1)
    chunk_of = lambda idx: jnp.right_shift(idx, ch.bit_length() - 1)
    block_of = lambda idx: jnp.right_shift(idx, GDN_INV_BLOCK.bit_length() - 1)
    same = chunk_of(row) == chunk_of(col)
    causal = jnp.logical_and(same, row >= col)
    strict = jnp.logical_and(same, row > col)
    blk = block_of(row) == block_of(col)
    eye = jnp.where(row == col, 1.0, 0.0).astype(F32)
    first = row < ch
    tri = jnp.where(causal, 1.0, 0.0).astype(F32)

    pairs = tt // pt
    units = [(p, h) for p in range(pairs) for h in range(GDN_HEADS)]
    rows = lambda p: slice(p * pt, (p + 1) * pt)
    per_unit = lambda fn: [fn(p, h) for p, h in units]
    bf = lambda xs: [x.astype(BF16) for x in xs]

    beta_all, gc_all, gc_all_t = [], [], []
    for p in range(pairs):
        ba = ba_ref[rows(p), :]
        beta_all.append(_sigmoid(ba))
        g = -jnp.exp(alog_ref[...]) * _softplus(ba + dtb_ref[...])
        gc = jnp.dot(tri, g, preferred_element_type=F32, precision=lax.Precision.HIGHEST)
        gc_all.append(gc)
        gc_all_t.append(gc.T)

    q16 = per_unit(lambda p, h: qkv_ref[rows(p), h * GDN_DK:(h + 1) * GDN_DK])
    k16 = per_unit(lambda p, h: qkv_ref[rows(p), GDN_QK_W + h * GDN_DK:GDN_QK_W + (h + 1) * GDN_DK])
    v16 = per_unit(lambda p, h: qkv_ref[rows(p),
                                        2 * GDN_QK_W + h * GDN_DV:2 * GDN_QK_W + (h + 1) * GDN_DV])
    q = [x.astype(F32) for x in q16]
    k = [x.astype(F32) for x in k16]
    v = [x.astype(F32) for x in v16]
    beta = per_unit(lambda p, h: _lane_bcast(beta_all[p], h))
    gcol = per_unit(lambda p, h: _lane_bcast(gc_all[p], GDN_HEADS + h))
    grow = per_unit(lambda p, h: _row_bcast(gc_all_t[p], GDN_HEADS + h, pt))
    decay = [jnp.where(causal, jnp.exp(jnp.where(causal, gi - gj, 0.0)), 0.0)
             for gi, gj in zip(gcol, grow)]
    egc = [jnp.exp(g) for g in gcol]
    kb = [ki * bi for ki, bi in zip(k, beta)]
    low = [jnp.where(strict, _mm_nt(a, b) * d, 0.0) for a, b, d in zip(bf(kb), k16, decay)]
    intra = bf([_mm_nt(a, b) * d for a, b, d in zip(q16, k16, decay)])
    rhs = bf([jnp.concatenate([vi * bi, kbi * ei], axis=1)
              for vi, bi, kbi, ei in zip(v, beta, kb, egc)])
    uw = bf(_unit_lower_solve(low, rhs, blk, eye))

    iu = [_mm(a, b) for a, b in zip(intra, uw)]
    o_const = [x[:, :GDN_DV] for x in iu]
    q_eff = bf([qi * ei - x[:, GDN_DV:] for qi, ei, x in zip(q, egc, iu)])
    glast = [jnp.where(first, _row_bcast(g, ch - 1, pt), _row_bcast(g, pt - 1, pt)) for g in gcol]
    kdec = [ki * jnp.exp(gl - g) for ki, gl, g in zip(k, glast, gcol)]
    chunks = [slice(c * ch, (c + 1) * ch) for c in range(pt // ch)]
    ku = [[_mm(kd[rs].T.astype(BF16), x[rs]) for rs in chunks] for kd, x in zip(kdec, uw)]

    s = [s_scr[h] for h in range(GDN_HEADS)]
    outs = {}
    for p in range(pairs):
        for c, rs in enumerate(chunks):
            for h in range(GDN_HEADS):
                i = p * GDN_HEADS + h
                sb = s[h].astype(BF16)
                outs[p, h, c] = _mm(q_eff[i][rs], sb) + o_const[i][rs]
                gl = _row_bcast(egc[i], rs.stop - 1, GDN_DK)
                s[h] = gl * s[h] + ku[i][c][:, :GDN_DV] - _mm(ku[i][c][:, GDN_DV:].astype(BF16), sb)
    for h in range(GDN_HEADS):
        s_scr[h] = s[h]

    gnorm = gnorm_ref[...]
    for p, h in units:
        hs = slice(h * GDN_DV, (h + 1) * GDN_DV)
        o = jnp.concatenate([outs[p, h, c] for c in range(len(chunks))], axis=0)
        o = o * lax.rsqrt(jnp.mean(o * o, axis=-1, keepdims=True) + RMS_EPS) * gnorm
        mix_scr[rows(p), hs] = (o * _silu(z_ref[rows(p), hs].astype(F32))).astype(mix_scr.dtype)

    tok = (t * tt + 1 + lax.broadcasted_iota(jnp.int32, (tt, POOL_GROUP_W), 0)).astype(F32)
    pscale = pscale_ref[...]
    for gi, win in enumerate(POOL_WINDOWS):
        cs = slice(gi * POOL_GROUP_W, (gi + 1) * POOL_GROUP_W)
        cur = p_ext[POOL_HALO:POOL_HALO + tt, cs]
        tot = cur
        for back in range(1, win):
            tot = tot + p_ext[pl.ds(POOL_HALO - back, tt), cs]
        pooled = tot / jnp.minimum(tok, float(win)) - cur
        y = _mm(pooled.astype(BF16), poolw_ref[gi])
        mix_scr[:, GDN_V_W + gi * POOL_GROUP_W:GDN_V_W + (gi + 1) * POOL_GROUP_W] = (
            y * pscale[:, cs]).astype(mix_scr.dtype)

    p_ext[0:POOL_HALO, :] = p_ext[tt:tt + POOL_HALO, :]

    gate = mod_ref[...][2:3]
    o_ref[...] = x_ref[...] + gate * _mm(mix_scr[...], wout_ref[...])


def _gdn_pool(qkv, z, p, ba, alog_row, dtb_row, gnorm, pool_w, pool_scale, w_out, layer, x, mod):
    b, t, d = x.shape
    tt = GDN_TILE
    row = lambda width: pl.BlockSpec((None, tt, width), lambda bi, i: (bi, i, 0))
    full = lambda a: pl.BlockSpec(a.shape, lambda bi, i: (0,) * a.ndim)
    small = (alog_row, dtb_row, gnorm, pool_w, pool_scale)
    return pl.pallas_call(
        _gdn_pool_kernel,
        grid=(b, t // tt),
        in_specs=[row(GDN_QKV_W), row(GDN_V_W), row(POOL_W), row(LANES)] + [full(a) for a in small] + [
            _resident(w_out, layer), row(d), pl.BlockSpec((None, 6, d), lambda bi, i: (bi, 0, 0))],
        out_specs=row(d),
        out_shape=jax.ShapeDtypeStruct((b, t, d), F32),
        scratch_shapes=[
            pltpu.VMEM((tt + POOL_HALO, POOL_W), F32),
            pltpu.VMEM((GDN_HEADS, GDN_DK, GDN_DV), F32),
            pltpu.VMEM((tt, GDN_V_W + POOL_W), BF16),
        ],
        compiler_params=_params(("parallel", "arbitrary")),
        name="gdn_pool_out_proj",
    )(qkv, z, p, ba, *small, w_out, x, mod)


def _out_res_kernel(a_ref, w_ref, x_ref, mod_ref, o_ref, *, gate_row):
    gate = mod_ref[...][gate_row:gate_row + 1]
    o_ref[...] = x_ref[...] + gate * _mm(a_ref[...], w_ref[...].astype(BF16))


def _out_res(a, w, layer, x, mod, gate_row):
    b, t, d = x.shape
    k = a.shape[-1]
    tm = 2 * ROW_TILE
    return pl.pallas_call(
        functools.partial(_out_res_kernel, gate_row=gate_row),
        grid=(b, t // tm),
        in_specs=[
            pl.BlockSpec((None, tm, k), lambda bi, i: (bi, i, 0)),
            _resident(w, layer),
            pl.BlockSpec((None, tm, d), lambda bi, i: (bi, i, 0)),
            pl.BlockSpec((None, 6, d), lambda bi, i: (bi, 0, 0)),
        ],
        out_specs=pl.BlockSpec((None, tm, d), lambda bi, i: (bi, i, 0)),
        out_shape=jax.ShapeDtypeStruct((b, t, d), F32),
        compiler_params=_params(("parallel", "parallel")),
        name="out_proj_residual",
    )(a, w, x, mod)


def _odd_in_kernel(x_ref, mod_ref, gain_ref, w_ref, qn_ref, kn_ref, o0_ref, o1_ref, o2_ref,
                   hn_slab, h_scr):
    tm = x_ref.shape[0]
    d = x_ref.shape[1]
    m = mod_ref[...]
    hn = _mod_norm(x_ref[...], gain_ref[...], m[0:1], m[1:2])
    for gi, (_, dil) in enumerate(DIL_PATTERNS):
        if dil == 1:
            h_scr[gi] = hn.astype(BF16)
    for sb in range(d // LANES):
        hn_slab[sb] = hn[:, sb * LANES:(sb + 1) * LANES]
    for gi, (_, dil) in enumerate(DIL_PATTERNS):
        per = tm // dil
        for r in range(dil if dil > 1 else 0):
            for sb in range(d // LANES):
                h_scr[gi, r * per:(r + 1) * per, sb * LANES:(sb + 1) * LANES] = (
                    hn_slab[sb, pl.ds(r, per, stride=dil), :].astype(BF16))
    qn = qn_ref[...] * (ATT_DH ** -0.5 * LOG2_E)
    kn = kn_ref[...]
    half = ATT_W // 2
    for gi, o_ref in enumerate((o0_ref, o1_ref, o2_ref)):
        dil = DIL_PATTERNS[gi][1]
        per = tm // dil
        for blk in range(3 * ATT_W // half):
            lo = blk * half
            y = _mm(h_scr[gi], w_ref[:, gi * 3 * ATT_W + lo:gi * 3 * ATT_W + lo + half])
            part = lo // ATT_W
            for h in range(half // ATT_DH):
                cs = slice(lo + h * ATT_DH, lo + (h + 1) * ATT_DH)
                yh = y[:, h * ATT_DH:(h + 1) * ATT_DH]
                if part < 2:
                    ms = jnp.mean(yh * yh, axis=-1, keepdims=True)
                    yh = yh * lax.rsqrt(ms + RMS_EPS) * (qn if part == 0 else kn)
                yh = yh.astype(o_ref.dtype)
                for r in range(dil):
                    o_ref[r, :, cs] = yh[r * per:(r + 1) * per]


def _odd_in(x, mod, gain, w, layer, q_norm, k_norm):
    b, t, d = x.shape
    tm = ROW_TILE
    dils = [dl for _, dl in DIL_PATTERNS]
    return pl.pallas_call(
        _odd_in_kernel,
        grid=(b, t // tm),
        in_specs=[
            pl.BlockSpec((None, tm, d), lambda bi, i: (bi, i, 0)),
            pl.BlockSpec((None, 6, d), lambda bi, i: (bi, 0, 0)),
            pl.BlockSpec((1, d), lambda bi, i: (0, 0)),
            _resident(w, layer),
            pl.BlockSpec((1, ATT_DH), lambda bi, i: (0, 0)),
            pl.BlockSpec((1, ATT_DH), lambda bi, i: (0, 0)),
        ],
        out_specs=[pl.BlockSpec((None, dl, tm // dl, 3 * ATT_W), lambda bi, i: (bi, 0, i, 0))
                   for dl in dils],
        out_shape=[jax.ShapeDtypeStruct((b, dl, t // dl, 3 * ATT_W), BF16) for dl in dils],
        scratch_shapes=[pltpu.VMEM((d // LANES, tm, LANES), F32),
                        pltpu.VMEM((len(dils), tm, d), BF16)],
        compiler_params=_params(("parallel", "parallel")),
        name="odd_in_proj",
    )(x, mod, gain, w, q_norm, k_norm)


def _attn_kernel(q_ref, kp_ref, kc_ref, vp_ref, vc_ref, o_ref, stat_ref, bias_scr, *, dil, n_back):
    blk = ATT_BLOCK
    first_step = jnp.logical_and(jnp.logical_and(pl.program_id(0) == 0, pl.program_id(1) == 0),
                                 pl.program_id(2) == 0)

    @pl.when(first_step)
    def _():
        a = lax.broadcasted_iota(jnp.int32, (blk, 2 * blk), 0)
        j = lax.broadcasted_iota(jnp.int32, (blk, 2 * blk), 1)
        rel = blk + a - j
        in_band = jnp.logical_and(rel >= 0, rel <= n_back)
        own = jnp.logical_and(in_band, j >= blk)
        relf = rel.astype(F32)
        for h in range(ATT_HEADS):
            bias = -(ATT_SLOPES[h] * dil * LOG2_E) * relf
            bias_scr[0, h] = jnp.where(own, bias, MASKED_SCORE)
            bias_scr[1, h] = jnp.where(in_band, bias, MASKED_SCORE)

    heads = [slice(h * ATT_DH, (h + 1) * ATT_DH) for h in range(ATT_HEADS)]
    lane = lax.broadcasted_iota(jnp.int32, (blk, LANES), 1)
    n_res, n_rows = q_ref.shape[0], q_ref.shape[1]
    for c, i in [(c, i) for c in range(n_res) for i in range(n_rows // blk)]:
        rows = slice(i * blk, (i + 1) * blk)
        if i == 0:
            variant = jnp.minimum(pl.program_id(2), 1)
            keys = lambda hs, c=c, rows=rows: jnp.concatenate([kp_ref[c, :, hs], kc_ref[c, rows, hs]],
                                                              axis=0)
            vals = lambda hs, c=c, rows=rows: jnp.concatenate([vp_ref[c, :, hs], vc_ref[c, rows, hs]],
                                                              axis=0)
        else:
            variant = 1
            both = slice((i - 1) * blk, (i + 1) * blk)
            keys = lambda hs, c=c, both=both: kc_ref[c, both, hs]
            vals = lambda hs, c=c, both=both: vc_ref[c, both, hs]
        s = [_mm_nt(q_ref[c, rows, hs], keys(hs)) + bias_scr[variant, h]
             for h, hs in enumerate(heads)]
        m = [jnp.max(sh, axis=-1, keepdims=True) for sh in s]
        p = [jnp.exp2(sh - mh) for sh, mh in zip(s, m)]
        l = [jnp.sum(ph, axis=-1, keepdims=True) for ph in p]
        o = [_mm(ph.astype(BF16), vals(hs)) for ph, hs in zip(p, heads)]
        stats = jnp.zeros((blk, LANES), F32)
        for h, hs in enumerate(heads):
            o_ref[c, rows, hs] = o[h].astype(o_ref.dtype)
            stats = jnp.where(lane == h, m[h], stats)
            stats = jnp.where(lane == ATT_HEADS + h, l[h], stats)
        stat_ref[c, rows, :] = stats


def _attention_group(qkv, window, dil):
    b, _, length, _ = qkv.shape
    qblocks = min(ATT_QBLOCKS, length // ATT_BLOCK)
    rows = qblocks * ATT_BLOCK
    n_res = min(dil, ATT_QBLOCKS // qblocks)
    assert length % rows == 0 and dil % n_res == 0

    def cur(part):
        return pl.BlockSpec((None, n_res, rows, ATT_W), lambda bi, r, n: (bi, r, n, part))

    def prev(part):
        return pl.BlockSpec((None, n_res, ATT_BLOCK, ATT_W),
                            lambda bi, r, n: (bi, r, jnp.maximum(n * qblocks - 1, 0), part))

    return pl.pallas_call(
        functools.partial(_attn_kernel, dil=dil, n_back=window // dil),
        grid=(b, dil // n_res, length // rows),
        in_specs=[cur(0), prev(1), cur(1), prev(2), cur(2)],
        out_specs=[
            pl.BlockSpec((None, n_res, rows, ATT_W), lambda bi, r, n: (bi, r, n, 0)),
            pl.BlockSpec((None, n_res, rows, LANES), lambda bi, r, n: (bi, r, n, 0)),
        ],
        out_shape=[
            jax.ShapeDtypeStruct((b, dil, length, ATT_W), BF16),
            jax.ShapeDtypeStruct((b, dil, length, LANES), F32),
        ],
        scratch_shapes=[pltpu.VMEM((2, ATT_HEADS, ATT_BLOCK, 2 * ATT_BLOCK), F32)],
        compiler_params=_params(("arbitrary", "arbitrary", "arbitrary")),
        name=f"dilated_attention_d{dil}",
    )(qkv, qkv, qkv, qkv, qkv)


def _merge_out_kernel(o0_ref, o1_ref, o2_ref, l0_ref, l1_ref, l2_ref, w_ref, x_ref, mod_ref,
                      out_ref, o_slab, l_slab, a_scr):
    tm = x_ref.shape[0]
    o_refs = (o0_ref, o1_ref, o2_ref)
    l_refs = (l0_ref, l1_ref, l2_ref)
    heads = [slice(h * ATT_DH, (h + 1) * ATT_DH) for h in range(ATT_HEADS)]
    for gi, (_, dil) in enumerate(DIL_PATTERNS):
        if dil == 1:
            continue
        for r in range(dil):
            dst = pl.ds(r, tm // dil, stride=dil)
            l_slab[gi, dst, :] = l_refs[gi][r]
            for h, hs in enumerate(heads):
                o_slab[gi, h, dst, :] = o_refs[gi][r, :, hs].astype(F32)

    def lse_of(gi):
        return l_refs[gi][0] if DIL_PATTERNS[gi][1] == 1 else l_slab[gi]

    def out_of(gi, h):
        if DIL_PATTERNS[gi][1] == 1:
            return o_refs[gi][0, :, heads[h]].astype(F32)
        return o_slab[gi, h]

    stats = [lse_of(gi) for gi in range(len(DIL_PATTERNS))]
    dens = [pltpu.roll(st, LANES - ATT_HEADS, axis=1) for st in stats]
    mx = jnp.maximum(jnp.maximum(stats[0], stats[1]), stats[2])
    es = [jnp.exp2(st - mx) for st in stats]
    inv = 1.0 / (es[0] * dens[0] + es[1] * dens[1] + es[2] * dens[2])
    wts = [e * inv for e in es]
    for h, hs in enumerate(heads):
        merged = (_lane_bcast(wts[0], h) * out_of(0, h) + _lane_bcast(wts[1], h) * out_of(1, h)
                  + _lane_bcast(wts[2], h) * out_of(2, h))
        a_scr[:, hs] = merged.astype(BF16)
    gate = mod_ref[...][2:3]
    out_ref[...] = x_ref[...] + gate * _mm(a_scr[...], w_ref[...])


def _merge_out(outs, lses, w, layer, x, mod):
    b, t, d = x.shape
    tm = 2 * ROW_TILE
    groups = len(DIL_PATTERNS)
    row = lambda width: pl.BlockSpec((None, tm, width), lambda bi, i: (bi, i, 0))
    by_residue = lambda dl, width: pl.BlockSpec((None, dl, tm // dl, width),
                                                lambda bi, i: (bi, 0, i, 0))
    dils = [dl for _, dl in DIL_PATTERNS]
    return pl.pallas_call(
        _merge_out_kernel,
        grid=(b, t // tm),
        in_specs=[by_residue(dl, ATT_W) for dl in dils] + [by_residue(dl, LANES) for dl in dils] + [
            _resident(w, layer), row(d), pl.BlockSpec((None, 6, d), lambda bi, i: (bi, 0, 0))],
        out_specs=row(d),
        out_shape=jax.ShapeDtypeStruct((b, t, d), F32),
        scratch_shapes=[
            pltpu.VMEM((groups, ATT_HEADS, tm, ATT_DH), F32),
            pltpu.VMEM((groups, tm, LANES), F32),
            pltpu.VMEM((tm, ATT_W), BF16),
        ],
        compiler_params=_params(("parallel", "parallel")),
        name="attn_merge_out_proj",
    )(*outs, *lses, w, x, mod)


def _ffn_up_kernel(x_ref, xh_ref, mod_ref, gain_ref, wg_ref, wv_ref, cw_ref, cb_ref, o_ref,
                   h_scr, g_scr, a_scr, *, chunk):
    tm = x_ref.shape[0]
    halo = FFN_HALO
    m = mod_ref[...]
    gain = gain_ref[...]
    shift, scale = m[3:4], m[4:5]
    h_scr[halo:, :] = _mod_norm(x_ref[...], gain, shift, scale).astype(BF16)
    live = jnp.where(pl.program_id(1) > 0, 1.0, 0.0)
    h_scr[0:halo, :] = (_mod_norm(xh_ref[...], gain, shift, scale) * live).astype(BF16)
    cw = cw_ref[...]
    cb = cb_ref[...]
    n_chunks = o_ref.shape[-1] // chunk
    for c in range(n_chunks):
        cs = slice(c * chunk, (c + 1) * chunk)
        g_scr[...] = _mm(h_scr[...], wg_ref[:, cs])
        gate = cb[:, cs] + cw[FFN_CONV - 1:FFN_CONV, cs] * g_scr[halo:, :]
        for j in range(FFN_CONV - 1):
            gate = gate + cw[j:j + 1, cs] * g_scr[pl.ds(halo - (FFN_CONV - 1) + j, tm), :]
        a_scr[:, cs] = _silu(gate)
    for c in range(n_chunks):
        cs = slice(c * chunk, (c + 1) * chunk)
        o_ref[:, cs] = (a_scr[:, cs] * _mm(h_scr[halo:, :], wv_ref[:, cs])).astype(o_ref.dtype)


def _ffn_up(x, mod, gain, w_up, layer, conv_w, conv_b):
    b, t, d = x.shape
    f = w_up.shape[-1] // 2
    tm = ROW_TILE
    chunk = 256
    per = tm // FFN_HALO
    half = lambda j: _resident(w_up, layer, col_blocks=2, col=j)
    return pl.pallas_call(
        functools.partial(_ffn_up_kernel, chunk=chunk),
        grid=(b, t // tm),
        in_specs=[
            pl.BlockSpec((None, tm, d), lambda bi, i: (bi, i, 0)),
            pl.BlockSpec((None, FFN_HALO, d), lambda bi, i: (bi, jnp.maximum(i * per - 1, 0), 0)),
            pl.BlockSpec((None, 6, d), lambda bi, i: (bi, 0, 0)),
            pl.BlockSpec((1, d), lambda bi, i: (0, 0)),
            half(0),
            half(1),
            pl.BlockSpec(conv_w.shape, lambda bi, i: (0, 0)),
            pl.BlockSpec(conv_b.shape, lambda bi, i: (0, 0)),
        ],
        out_specs=pl.BlockSpec((None, tm, f), lambda bi, i: (bi, i, 0)),
        out_shape=jax.ShapeDtypeStruct((b, t, f), BF16),
        scratch_shapes=[pltpu.VMEM((tm + FFN_HALO, d), BF16), pltpu.VMEM((tm + FFN_HALO, chunk), F32),
                        pltpu.VMEM((tm, f), F32)],
        compiler_params=_params(("parallel", "parallel")),
        name="ffn_up_conv_gate",
    )(x, x, mod, gain, w_up, w_up, conv_w, conv_b)


def _lane_row(values, offset):
    return jnp.zeros((1, LANES), F32).at[0, offset:offset + values.shape[0]].set(values)


def kernel(x, c, ada_w, ada_b, norm_mix, norm_ffn, ev_w_in, ev_w_out, gdn_conv_w, gdn_a_log,
           gdn_dt_bias, gdn_norm, pool_w, pool_scale, od_w_in, od_w_out, att_q_norm, att_k_norm,
           ffn_w_up, ffn_conv_w, ffn_conv_b, ffn_w_down):
    b, t, d = x.shape
    depth = ada_w.shape[0]
    assert t % (ATT_BLOCK * max(dl for _, dl in DIL_PATTERNS)) == 0 and t % (2 * ROW_TILE) == 0
    d_ff = ffn_w_down.shape[1]

    mod = _ada(c, ada_w, ada_b)

    i1 = GDN_QKV_W + GDN_V_W
    i2 = i1 + 2 * GDN_HEADS
    pad = jnp.zeros(ev_w_in.shape[:2] + (LANES - 2 * GDN_HEADS,), ev_w_in.dtype)
    ev_w = jnp.concatenate([ev_w_in[..., :i1], ev_w_in[..., i2:], ev_w_in[..., i1:i2], pad],
                           axis=-1).astype(BF16)
    ev_wo, od_w, od_wo = ev_w_out.astype(BF16), od_w_in.astype(BF16), od_w_out.astype(BF16)
    up_w, down_w = ffn_w_up.astype(BF16), ffn_w_down

    for i in range(depth):
        m = mod[i]
        gain_m = norm_mix[i].reshape(1, d)
        if i % 2 == 0:
            e = i // 2
            qkv, z, p, ba = _even_in(x, m, gain_m, ev_w, e, gdn_conv_w[e])
            x = _gdn_pool(
                qkv, z, p, ba,
                _lane_row(gdn_a_log[e], GDN_HEADS), _lane_row(gdn_dt_bias[e], GDN_HEADS),
                gdn_norm[e].reshape(1, GDN_DV), pool_w[e].astype(BF16),
                pool_scale[e].reshape(1, POOL_W), ev_wo, e, x, m)
        else:
            o = i // 2
            qkvs = _odd_in(x, m, gain_m, od_w, o,
                           att_q_norm[o].reshape(1, ATT_DH), att_k_norm[o].reshape(1, ATT_DH))
            outs, lses = [], []
            for qkv, (window, dil) in zip(qkvs, DIL_PATTERNS):
                og, lg = _attention_group(qkv, window, dil)
                outs.append(og)
                lses.append(lg)
            x = _merge_out(outs, lses, od_wo, o, x, m)

        act = _ffn_up(x, m, norm_ffn[i].reshape(1, d), up_w, i,
                      ffn_conv_w[i], ffn_conv_b[i].reshape(1, d_ff))
        x = _out_res(act, down_w, i, x, m, 5)
    return x
```
